```python
import jax
import jax.numpy as jnp
from jax import lax
import numpy as np

D_MODEL = 4096
BATCH = 4
SEQ = 2048
DEPTH = 1
DEC_BATCH = 128
DEC_SEQ = 4
PAST_LEN = 2048
PAGE_SIZE = 128

D_CONV = D_MODEL // 2
CONV_WIDTH = 3
HEAD_DIM = 128
N_HEADS = (D_MODEL // 2) // HEAD_DIM
N_KV_HEADS = N_HEADS // 4
GROUP = N_HEADS // N_KV_HEADS
D_ATTN = N_HEADS * HEAD_DIM
D_KV = N_KV_HEADS * HEAD_DIM
ROT_DIM = HEAD_DIM // 4
ROPE_THETA = 500000.0
MOBA_BLOCK = 256
MOBA_TOPK = 3
Q_CHUNK = 16
N_GROUPS = 8
EXPERTS_PER_GROUP = 8
N_EXPERTS = N_GROUPS * EXPERTS_PER_GROUP
TOPK_IN_GROUP = 2
D_EXPERT = D_MODEL // 4
ROW_BLOCK = 128
ALPHA = (2 * DEPTH) ** 0.25
BETA = (8 * DEPTH) ** -0.25
LN_EPS = 1e-5
SPLIT_POINTS = (D_CONV, 2 * D_CONV, 3 * D_CONV, 3 * D_CONV + D_ATTN,
                3 * D_CONV + D_ATTN + D_KV, 3 * D_CONV + D_ATTN + 2 * D_KV,
                3 * D_CONV + D_ATTN + 2 * D_KV + D_MODEL)
D_IN_TOTAL = 3 * D_CONV + D_ATTN + 2 * D_KV + 2 * D_MODEL

kernel_name = 'hybrid_conv_moba_hmoe_step'


def layer_norm(x, g, b):
    xf = x.astype(jnp.float32)
    mu = jnp.mean(xf, -1, keepdims=True)
    var = jnp.mean(jnp.square(xf - mu), -1, keepdims=True)
    return ((xf - mu) * lax.rsqrt(var + LN_EPS) * g.astype(jnp.float32) + b.astype(jnp.float32)).astype(x.dtype)


def rope_partial(x, pos):
    half = ROT_DIM // 2
    inv = ROPE_THETA ** (-jnp.arange(half, dtype=jnp.float32) / half)
    ang = pos.astype(jnp.float32)[:, None] * inv[None, :]
    cos = jnp.cos(ang)[None, :, None, :]
    sin = jnp.sin(ang)[None, :, None, :]
    xr = x[..., :ROT_DIM].astype(jnp.float32)
    x1, x2 = xr[..., :half], xr[..., half:]
    rot = jnp.concatenate([x1 * cos - x2 * sin, x2 * cos + x1 * sin], -1).astype(x.dtype)
    return jnp.concatenate([rot, x[..., ROT_DIM:]], -1)


def short_conv(u, u_prev, w):
    T = u.shape[1]
    up = jnp.concatenate([u_prev, u], 1)
    y = sum(w[j] * up[:, j:j + T] for j in range(CONV_WIDTH))
    return y, up[:, up.shape[1] - (CONV_WIDTH - 1):]


def moba_attention(q, q_pos, k_all, v_all, q_chunk):
    B, T = q.shape[:2]
    L = k_all.shape[1]
    nb = -(-L // MOBA_BLOCK)
    pad = nb * MOBA_BLOCK - L
    padw = ((0, 0), (0, pad), (0, 0), (0, 0))
    kb = jnp.pad(k_all, padw).reshape(B, nb, MOBA_BLOCK, N_KV_HEADS, HEAD_DIM).transpose(0, 3, 1, 2, 4)
    vb = jnp.pad(v_all, padw).reshape(B, nb, MOBA_BLOCK, N_KV_HEADS, HEAD_DIM).transpose(0, 3, 1, 2, 4)
    n_slots = max(nb, MOBA_TOPK)
    kmean = jnp.mean(kb.astype(jnp.float32), axis=3)
    kmean = jnp.pad(kmean, ((0, 0), (0, 0), (0, n_slots - nb), (0, 0)))
    n_chunks = T // q_chunk
    qc = q.reshape(B, n_chunks, q_chunk, N_KV_HEADS, GROUP, HEAD_DIM).transpose(1, 0, 2, 3, 4, 5)
    pc = q_pos.reshape(n_chunks, q_chunk)
    b_ix = jnp.arange(B)[:, None, None, None, None]
    h_ix = jnp.arange(N_KV_HEADS)[None, None, :, None, None]
    blk_ids = jnp.arange(n_slots)
    key_off = jnp.arange(MOBA_BLOCK)
    scale = HEAD_DIM ** -0.5

    def one_chunk(args):
        qq, pp = args
        own = pp // MOBA_BLOCK
        gate = jnp.einsum('btkgd,bknd->btkgn', qq.astype(jnp.float32), kmean)
        fully_past = blk_ids[None, :] < own[:, None]
        gate = jnp.where(fully_past[None, :, None, None, :], gate, -jnp.inf)
        _, top = lax.top_k(gate, MOBA_TOPK)
        valid = top < own[None, :, None, None, None]
        own_b = jnp.broadcast_to(own[None, :, None, None, None], top.shape[:-1] + (1,))
        sel = jnp.concatenate([jnp.minimum(top, nb - 1), own_b], -1)
        kg = kb[b_ix, h_ix, sel]
        vg = vb[b_ix, h_ix, sel]
        s = jnp.einsum('btkgd,btkgsjd->btkgsj', qq, kg, preferred_element_type=jnp.float32) * scale
        causal = (own[:, None] * MOBA_BLOCK + key_off[None, :]) <= pp[:, None]
        mask = jnp.concatenate([
            jnp.broadcast_to(valid[..., None], valid.shape + (MOBA_BLOCK,)),
            jnp.broadcast_to(causal[None, :, None, None, None, :], valid.shape[:-1] + (1, MOBA_BLOCK))], axis=4)
        s = jnp.where(mask, s, -jnp.inf)
        p = jax.nn.softmax(s.reshape(s.shape[:4] + (-1,)), axis=-1).reshape(s.shape)
        o = jnp.einsum('btkgsj,btkgsjd->btkgd', p.astype(vg.dtype), vg, preferred_element_type=jnp.float32)
        return o.astype(q.dtype)

    out = lax.map(one_chunk, (qc, pc))
    return out.transpose(1, 0, 2, 3, 4, 5).reshape(B, T, D_ATTN)


def token_mixer(x, pos, conv_prev, k_past, v_past, w_in, conv_w, w_conv_branch, w_attn_branch, w_o, q_chunk):
    B, T, _ = x.shape
    h = x @ w_in
    cb, cc, cx, q, k, v, g_conv, g_attn = jnp.split(h, SPLIT_POINTS, axis=-1)
    u = cc * cx
    conv_y, conv_tail = short_conv(u, conv_prev, conv_w)
    y_a = (cb * conv_y) @ w_conv_branch
    q = rope_partial(q.reshape(B, T, N_HEADS, HEAD_DIM), pos)
    k = rope_partial(k.reshape(B, T, N_KV_HEADS, HEAD_DIM), pos)
    v = v.reshape(B, T, N_KV_HEADS, HEAD_DIM)
    if k_past is None:
        k_all, v_all = k, v
    else:
        k_all = jnp.concatenate([k_past, k], 1)
        v_all = jnp.concatenate([v_past, v], 1)
    y_b = moba_attention(q, pos, k_all, v_all, q_chunk) @ w_attn_branch
    merged = jax.nn.sigmoid(g_conv) * y_a + jax.nn.sigmoid(g_attn) * y_b
    return merged @ w_o, conv_tail, k, v


def hier_moe(h, w_rg, b_rg, w_re, b_re, w_gate, w_up, w_down):
    N, D = h.shape
    lg = (h @ w_rg).astype(jnp.float32) + b_rg.astype(jnp.float32)
    pg = jax.nn.softmax(lg, -1)
    grp = jnp.argmax(lg, -1)
    wg = jnp.take_along_axis(pg, grp[:, None], -1)
    le = (h @ w_re).astype(jnp.float32) + b_re.astype(jnp.float32)
    le = le.reshape(N, N_GROUPS, EXPERTS_PER_GROUP)
    le = jnp.take_along_axis(le, grp[:, None, None], 1)[:, 0]
    top_l, top = lax.top_k(le, TOPK_IN_GROUP)
    weights = wg * jax.nn.softmax(top_l, -1)
    eid = grp[:, None] * EXPERTS_PER_GROUP + top
    A = N * TOPK_IN_GROUP
    flat_e = eid.reshape(-1).astype(jnp.int32)
    flat_t = jnp.repeat(jnp.arange(N, dtype=jnp.int32), TOPK_IN_GROUP)
    flat_w = weights.reshape(-1)
    order = jnp.argsort(flat_e)
    se = flat_e[order]
    counts = jnp.bincount(flat_e, length=N_EXPERTS).astype(jnp.int32)
    padded = (counts + ROW_BLOCK - 1) // ROW_BLOCK * ROW_BLOCK
    start = jnp.cumsum(counts) - counts
    pend = jnp.cumsum(padded)
    pstart = pend - padded
    dest = pstart[se] + jnp.arange(A, dtype=jnp.int32) - start[se]
    n_blocks = -(-A // ROW_BLOCK) + N_EXPERTS
    P = n_blocks * ROW_BLOCK
    row_tok = jnp.full((P,), N, jnp.int32).at[dest].set(flat_t[order])
    row_w = jnp.zeros((P,), h.dtype).at[dest].set(flat_w[order].astype(h.dtype))
    blk_e = jnp.minimum(jnp.searchsorted(pend, jnp.arange(n_blocks, dtype=jnp.int32) * ROW_BLOCK, side='right'), N_EXPERTS - 1)
    h_pad = jnp.concatenate([h, jnp.zeros((1, D), h.dtype)], 0)
    xs = h_pad[row_tok].reshape(n_blocks, ROW_BLOCK, D)

    def expert_block(args):
        xb, e = args
        return (jax.nn.silu(xb @ w_gate[e]) * (xb @ w_up[e])) @ w_down[e]

    ys = lax.map(expert_block, (xs, blk_e)).reshape(P, D)
    out = jnp.zeros((N + 1, D), h.dtype).at[row_tok].add(ys * row_w[:, None])
    return out[:N]


def decoder_layer(x_p, x_s, k_cache, v_cache, conv_state, page_table, w_in, conv_w, w_conv_branch,
                  w_attn_branch, w_o, ln1_g, ln1_b, w_rg, b_rg, w_re, b_re, w_gate, w_up, w_down, ln2_g, ln2_b):
    B, S, D = x_p.shape
    DB, T, _ = x_s.shape
    past = page_table.shape[1] * k_cache.shape[1]
    pos_p = jnp.arange(S, dtype=jnp.int32)
    zeros_conv = jnp.zeros((B, CONV_WIDTH - 1, D_CONV), x_p.dtype)
    mix_p, conv_p, k_p, v_p = token_mixer(x_p, pos_p, zeros_conv, None, None, w_in, conv_w,
                                          w_conv_branch, w_attn_branch, w_o, Q_CHUNK)
    pos_s = past + jnp.arange(T, dtype=jnp.int32)
    k_past = k_cache[page_table].reshape(DB, past, N_KV_HEADS, HEAD_DIM)
    v_past = v_cache[page_table].reshape(DB, past, N_KV_HEADS, HEAD_DIM)
    mix_s, conv_s, k_s, v_s = token_mixer(x_s, pos_s, conv_state, k_past, v_past, w_in, conv_w,
                                          w_conv_branch, w_attn_branch, w_o, 1)
    h = jnp.concatenate([(ALPHA * x_p + mix_p).reshape(B * S, D),
                         (ALPHA * x_s + mix_s).reshape(DB * T, D)], 0)
    h = layer_norm(h, ln1_g, ln1_b)
    z = layer_norm(ALPHA * h + hier_moe(h, w_rg, b_rg, w_re, b_re, w_gate, w_up, w_down), ln2_g, ln2_b)
    return (z[:B * S].reshape(B, S, D), z[B * S:].reshape(DB, T, D), k_p, v_p, conv_p, k_s, v_s, conv_s)


def setup_inputs(seed: int = 0) -> dict:
    key = jax.random.key(seed)
    ks = jax.random.split(key, 24)
    n_pages = PAST_LEN // PAGE_SIZE
    used = DEC_BATCH * n_pages
    n_pool = used + max(1, used // 4)

    def nrm(k, shape, scale):
        return jax.random.normal(k, shape, jnp.float32) * scale

    page_table = jax.random.permutation(ks[5], n_pool)[:used].reshape(DEC_BATCH, n_pages).astype(jnp.int32)
    return {
        'x_prompt': nrm(ks[0], (BATCH, SEQ, D_MODEL), 1.0),
        'x_sample': nrm(ks[1], (DEC_BATCH, DEC_SEQ, D_MODEL), 1.0),
        'cache_k': nrm(ks[2], (DEPTH, n_pool, PAGE_SIZE, N_KV_HEADS, HEAD_DIM), 1.0),
        'cache_v': nrm(ks[3], (DEPTH, n_pool, PAGE_SIZE, N_KV_HEADS, HEAD_DIM), 1.0),
        'state_conv': nrm(ks[4], (DEPTH, DEC_BATCH, CONV_WIDTH - 1, D_CONV), 1.0),
        'page_table': page_table,
        'w_in': nrm(ks[6], (DEPTH, D_MODEL, D_IN_TOTAL), D_MODEL ** -0.5),
        'conv_w': nrm(ks[7], (DEPTH, CONV_WIDTH, D_CONV), CONV_WIDTH ** -0.5),
        'w_conv_branch': nrm(ks[8], (DEPTH, D_CONV, D_MODEL), D_CONV ** -0.5),
        'w_attn_branch': nrm(ks[9], (DEPTH, D_ATTN, D_MODEL), D_ATTN ** -0.5),
        'w_o': nrm(ks[10], (DEPTH, D_MODEL, D_MODEL), BETA * D_MODEL ** -0.5),
        'ln1_g': 1.0 + nrm(ks[11], (DEPTH, D_MODEL), 0.02),
        'ln1_b': nrm(ks[12], (DEPTH, D_MODEL), 0.02),
        'w_router_group': nrm(ks[13], (DEPTH, D_MODEL, N_GROUPS), D_MODEL ** -0.5),
        'b_router_group': nrm(ks[14], (DEPTH, N_GROUPS), 0.01),
        'w_router_expert': nrm(ks[15], (DEPTH, D_MODEL, N_EXPERTS), D_MODEL ** -0.5),
        'b_router_expert': nrm(ks[16], (DEPTH, N_EXPERTS), 0.01),
        'w_gate': nrm(ks[17], (DEPTH, N_EXPERTS, D_MODEL, D_EXPERT), D_MODEL ** -0.5),
        'w_up': nrm(ks[18], (DEPTH, N_EXPERTS, D_MODEL, D_EXPERT), D_MODEL ** -0.5),
        'w_down': nrm(ks[19], (DEPTH, N_EXPERTS, D_EXPERT, D_MODEL), BETA * D_EXPERT ** -0.5),
        'ln2_g': 1.0 + nrm(ks[20], (DEPTH, D_MODEL), 0.02),
        'ln2_b': nrm(ks[21], (DEPTH, D_MODEL), 0.02),
    }


def reference(x_prompt, x_sample, cache_k, cache_v, state_conv, page_table, w_in, conv_w, w_conv_branch,
              w_attn_branch, w_o, ln1_g, ln1_b, w_router_group, b_router_group, w_router_expert,
              b_router_expert, w_gate, w_up, w_down, ln2_g, ln2_b):
    x_p, x_s = x_prompt, x_sample
    kp_l, vp_l, cp_l, ks_l, vs_l, cs_l = [], [], [], [], [], []
    for l in range(DEPTH):
        x_p, x_s, kp, vp, cp, k_s, v_s, c_s = decoder_layer(
            x_p, x_s, cache_k[l], cache_v[l], state_conv[l], page_table, w_in[l], conv_w[l],
            w_conv_branch[l], w_attn_branch[l], w_o[l], ln1_g[l], ln1_b[l], w_router_group[l],
            b_router_group[l], w_router_expert[l], b_router_expert[l], w_gate[l], w_up[l], w_down[l],
            ln2_g[l], ln2_b[l])
        kp_l.append(kp)
        vp_l.append(vp)
        cp_l.append(cp)
        ks_l.append(k_s)
        vs_l.append(v_s)
        cs_l.append(c_s)
    return (x_p, x_s, jnp.stack(kp_l), jnp.stack(vp_l), jnp.stack(cp_l), jnp.stack(ks_l), jnp.stack(vs_l), jnp.stack(cs_l))
```

```python
import functools

import jax
import jax.numpy as jnp
from jax import lax
from jax.experimental import pallas as pl
from jax.experimental.pallas import tpu as pltpu

D_MODEL = 4096
BATCH = 4
SEQ = 2048
DEC_BATCH = 128
DEC_SEQ = 4
PAST_LEN = 2048
PAGE_SIZE = 128
N_PAGES = PAST_LEN // PAGE_SIZE
D_CONV = D_MODEL // 2
CONV_WIDTH = 3
HEAD_DIM = 128
N_HEADS = 16
N_KV_HEADS = 4
GROUP = N_HEADS // N_KV_HEADS
D_ATTN = N_HEADS * HEAD_DIM
D_KV = N_KV_HEADS * HEAD_DIM
ROT_DIM = HEAD_DIM // 4
ROPE_THETA = 500000.0
MOBA_BLOCK = 256
MOBA_TOPK = 3
N_GROUPS = 8
EXPERTS_PER_GROUP = 8
N_EXPERTS = N_GROUPS * EXPERTS_PER_GROUP
TOPK_IN_GROUP = 2
D_EXPERT = D_MODEL // 4
DEPTH = 1
ALPHA = (2 * DEPTH) ** 0.25
LN_EPS = 1e-5
D_IN_TOTAL = 3 * D_CONV + D_ATTN + 2 * D_KV + 2 * D_MODEL

N_P = BATCH * SEQ
N_S = DEC_BATCH * DEC_SEQ
N_TOK = N_P + N_S
N_ASSIGN = N_TOK * TOPK_IN_GROUP

COL_CB, COL_CC, COL_CX = 0, D_CONV, 2 * D_CONV
COL_Q = 3 * D_CONV
COL_K = COL_Q + D_ATTN
COL_V = COL_K + D_KV
COL_GC = COL_V + D_KV
COL_GA = COL_GC + D_MODEL

TILE = 512
MOE_TM = 256
MOE_NB = -(-N_ASSIGN // MOE_TM) + N_EXPERTS
MOE_P = MOE_NB * MOE_TM
MOE_TF = 512
MOE_TN = 1024
LANES = 128
VMEM_LIMIT = 56 * 1024 * 1024

BF16 = jnp.bfloat16
F32 = jnp.float32
_NT = (((1,), (1,)), ((), ()))


def _params(sem, vmem=VMEM_LIMIT):
    return pltpu.CompilerParams(dimension_semantics=sem, vmem_limit_bytes=vmem)


def _sigmoid(x):
    return 1.0 / (1.0 + jnp.exp(-x))


def _in_proj_kernel(x_ref, w_ref, o_ref, wbf_ref):
    @pl.when(pl.program_id(1) == 0)
    def _():
        wbf_ref[...] = w_ref[...].astype(BF16)

    o_ref[...] = jnp.dot(x_ref[...], wbf_ref[...], preferred_element_type=F32)


def _in_proj(x_bf, w_in):
    return pl.pallas_call(
        _in_proj_kernel,
        grid=(D_IN_TOTAL // TILE, N_TOK // TILE),
        in_specs=[pl.BlockSpec((TILE, D_MODEL), lambda j, i: (i, 0)),
                  pl.BlockSpec((D_MODEL, TILE), lambda j, i: (0, j))],
        out_specs=pl.BlockSpec((TILE, TILE), lambda j, i: (i, j)),
        out_shape=jax.ShapeDtypeStruct((N_TOK, D_IN_TOTAL), F32),
        scratch_shapes=[pltpu.VMEM((D_MODEL, TILE), BF16)],
        compiler_params=_params(("arbitrary", "arbitrary")),
        name="in_proj",
    )(x_bf, w_in)


def _conv_prompt_kernel(cb_ref, cc_ref, cx_ref, w_ref, a_ref, tail_ref, prev_ref):
    @pl.when(pl.program_id(2) == 0)
    def _():
        prev_ref[...] = jnp.zeros_like(prev_ref)

    u = cc_ref[...] * cx_ref[...]
    rows = lax.broadcasted_iota(jnp.int32, u.shape, 0)
    p0 = prev_ref[0:1, :]
    p1 = prev_ref[1:2, :]
    um1 = jnp.where(rows == 0, p1, pltpu.roll(u, 1, 0))
    um2 = jnp.where(rows == 0, p0, jnp.where(rows == 1, p1, pltpu.roll(u, 2, 0)))
    y = w_ref[0:1, :] * um2 + w_ref[1:2, :] * um1 + w_ref[2:3, :] * u
    a_ref[...] = (cb_ref[...] * y).astype(BF16)
    last = u[TILE - 2:TILE, :]
    prev_ref[0:2, :] = last
    tail_ref[...] = last


def _conv_prompt(h, conv_w):
    nr = SEQ // TILE
    nc = D_CONV // TILE

    def sec(col):
        return pl.BlockSpec((TILE, TILE), lambda b, c, r, col=col: (b * nr + r, col // TILE + c))

    return pl.pallas_call(
        _conv_prompt_kernel,
        grid=(BATCH, nc, nr),
        in_specs=[sec(COL_CB), sec(COL_CC), sec(COL_CX),
                  pl.BlockSpec((CONV_WIDTH, TILE), lambda b, c, r: (0, c))],
        out_specs=[pl.BlockSpec((TILE, TILE), lambda b, c, r: (b * nr + r, c)),
                   pl.BlockSpec((None, CONV_WIDTH - 1, TILE), lambda b, c, r: (b, 0, c))],
        out_shape=[jax.ShapeDtypeStruct((N_TOK, D_CONV), BF16),
                   jax.ShapeDtypeStruct((BATCH, CONV_WIDTH - 1, D_CONV), F32)],
        scratch_shapes=[pltpu.VMEM((8, TILE), F32)],
        compiler_params=_params(("arbitrary", "arbitrary", "arbitrary")),
        name="conv_prompt",
    )(h, h, h, conv_w)


def _conv_sample_kernel(cb_ref, cc_ref, cx_ref, st_ref, w_ref, a_ref, tail_ref):
    b = DEC_BATCH
    w0, w1, w2 = w_ref[0:1, :], w_ref[1:2, :], w_ref[2:3, :]
    u = cc_ref[...] * cx_ref[...]
    up = [st_ref[0], st_ref[1]] + [u[t * b:(t + 1) * b, :] for t in range(DEC_SEQ)]
    for t in range(DEC_SEQ):
        y = w0 * up[t] + w1 * up[t + 1] + w2 * up[t + 2]
        a_ref[t * b:(t + 1) * b, :] = (cb_ref[t * b:(t + 1) * b, :] * y).astype(BF16)
    tail_ref[0] = up[DEC_SEQ]
    tail_ref[1] = up[DEC_SEQ + 1]


def _conv_sample(h, state_t, conv_w):
    nc = D_CONV // TILE
    rb = N_P // N_S

    def sec(col):
        return pl.BlockSpec((N_S, TILE), lambda c, col=col: (rb, col // TILE + c))

    return pl.pallas_call(
        _conv_sample_kernel,
        grid=(nc,),
        in_specs=[sec(COL_CB), sec(COL_CC), sec(COL_CX),
                  pl.BlockSpec((CONV_WIDTH - 1, DEC_BATCH, TILE), lambda c: (0, 0, c)),
                  pl.BlockSpec((CONV_WIDTH, TILE), lambda c: (0, c))],
        out_specs=[pl.BlockSpec((N_S, TILE), lambda c: (0, c)),
                   pl.BlockSpec((CONV_WIDTH - 1, DEC_BATCH, TILE), lambda c: (0, 0, c))],
        out_shape=[jax.ShapeDtypeStruct((N_S, D_CONV), BF16),
                   jax.ShapeDtypeStruct((CONV_WIDTH - 1, DEC_BATCH, D_CONV), F32)],
        compiler_params=_params(("arbitrary",)),
        name="conv_sample",
    )(h, h, h, state_t, conv_w)


def _rope_kernel(q_ref, k_ref, v_ref, c_ref, s1_ref, s2_ref, qo_ref, ko_ref, vo_ref):
    c, s1, s2 = c_ref[...], s1_ref[...], s2_ref[...]

    def rot(x):
        return x * c + pltpu.roll(x, LANES - ROT_DIM // 2, 1) * s1 + pltpu.roll(x, ROT_DIM // 2, 1) * s2

    for hd in range(N_HEADS):
        sl = slice(hd * HEAD_DIM, (hd + 1) * HEAD_DIM)
        qo_ref[:, sl] = rot(q_ref[:, sl])
    for hd in range(N_KV_HEADS):
        sl = slice(hd * HEAD_DIM, (hd + 1) * HEAD_DIM)
        ko_ref[:, sl] = rot(k_ref[:, sl])
    vo_ref[...] = v_ref[...]


def _rope(h, tab_c, tab_s1, tab_s2):
    n_prompt_tiles = N_P // TILE
    tiles_per_seq = SEQ // TILE

    def tab_map(i):
        return (jnp.where(i < n_prompt_tiles, i % tiles_per_seq, tiles_per_seq), 0)

    tab = pl.BlockSpec((TILE, LANES), tab_map)
    return pl.pallas_call(
        _rope_kernel,
        grid=(N_TOK // TILE,),
        in_specs=[pl.BlockSpec((TILE, D_ATTN), lambda i: (i, COL_Q // D_ATTN)),
                  pl.BlockSpec((TILE, D_KV), lambda i: (i, COL_K // D_KV)),
                  pl.BlockSpec((TILE, D_KV), lambda i: (i, COL_V // D_KV)),
                  tab, tab, tab],
        out_specs=[pl.BlockSpec((TILE, D_ATTN), lambda i: (i, 0)),
                   pl.BlockSpec((TILE, D_KV), lambda i: (i, 0)),
                   pl.BlockSpec((TILE, D_KV), lambda i: (i, 0))],
        out_shape=[jax.ShapeDtypeStruct((N_TOK, D_ATTN), F32),
                   jax.ShapeDtypeStruct((N_TOK, D_KV), F32),
                   jax.ShapeDtypeStruct((N_TOK, D_KV), F32)],
        compiler_params=_params(("arbitrary",)),
        name="rope",
    )(h, h, h, tab_c, tab_s1, tab_s2)


def _rope_tables():
    half = ROT_DIM // 2
    inv = ROPE_THETA ** (-jnp.arange(half, dtype=F32) / half)
    pos = jnp.concatenate([jnp.arange(SEQ, dtype=jnp.int32),
                           PAST_LEN + jnp.repeat(jnp.arange(DEC_SEQ, dtype=jnp.int32), DEC_BATCH)])
    ang = pos.astype(F32)[:, None] * inv[None, :]
    cos, sin = jnp.cos(ang), jnp.sin(ang)
    n = pos.shape[0]
    ones = jnp.ones((n, HEAD_DIM - ROT_DIM), F32)
    zeros = jnp.zeros((n, HEAD_DIM - ROT_DIM), F32)
    zh = jnp.zeros((n, half), F32)
    tab_c = jnp.concatenate([cos, cos, ones], 1)
    tab_s1 = jnp.concatenate([-sin, zh, zeros], 1)
    tab_s2 = jnp.concatenate([zh, sin, zeros], 1)
    return tab_c, tab_s1, tab_s2


def _topk_blocks(gate, allowed, lane_blk):
    gate = jnp.where(allowed, gate, -jnp.inf)
    rank = jnp.zeros(gate.shape, jnp.int32)
    for m in range(gate.shape[1]):
        gm = gate[:, m:m + 1]
        beats = (gm > gate) | ((gm == gate) & (lane_blk > m))
        rank = rank + beats.astype(jnp.int32)
    return (rank < MOBA_TOPK) & allowed


def _attn_prompt_kernel(q_ref, k_ref, v_ref, o_ref, kmean_ref, kbf_ref, vbf_ref):
    nb = SEQ // MOBA_BLOCK
    qi = pl.program_id(2)

    @pl.when(qi == 0)
    def _():
        k = k_ref[...]
        kmean_ref[...] = jnp.mean(k.reshape(nb, MOBA_BLOCK, HEAD_DIM), axis=1)
        kbf_ref[...] = k.astype(BF16)
        vbf_ref[...] = v_ref[...].astype(BF16)

    kmean = kmean_ref[...]
    kbf = kbf_ref[...]
    vbf = vbf_ref[...]
    lane_blk = lax.broadcasted_iota(jnp.int32, (MOBA_BLOCK, nb), 1)
    past = lane_blk < qi
    tri = (lax.broadcasted_iota(jnp.int32, (MOBA_BLOCK, MOBA_BLOCK), 1)
           <= lax.broadcasted_iota(jnp.int32, (MOBA_BLOCK, MOBA_BLOCK), 0)).astype(jnp.int32)
    scale = HEAD_DIM ** -0.5
    for g in range(GROUP):
        sl = slice(g * HEAD_DIM, (g + 1) * HEAD_DIM)
        q = q_ref[:, sl]
        gate = lax.dot_general(q, kmean, _NT, precision=lax.Precision.HIGHEST, preferred_element_type=F32)
        sel = _topk_blocks(gate, past, lane_blk).astype(jnp.int32)
        s = lax.dot_general(q.astype(BF16), kbf, _NT, preferred_element_type=F32) * scale
        pieces = []
        for n in range(nb):
            own = (qi == n).astype(jnp.int32)
            seln = jnp.broadcast_to(sel[:, n:n + 1], (MOBA_BLOCK, MOBA_BLOCK))
            pieces.append(seln + own * tri)
        mask = jnp.concatenate(pieces, axis=1) > 0
        s = jnp.where(mask, s, -jnp.inf)
        m = jnp.max(s, axis=-1, keepdims=True)
        p = jnp.exp(s - m)
        l = jnp.sum(p, axis=-1, keepdims=True)
        o = jnp.dot(p.astype(BF16), vbf, preferred_element_type=F32) / l
        o_ref[:, sl] = o.astype(BF16)


def _attn_prompt(q_rot, k_rot, v):
    nq = SEQ // MOBA_BLOCK
    qw = GROUP * HEAD_DIM
    kv_spec = pl.BlockSpec((SEQ, HEAD_DIM), lambda b, kv, qi: (b, kv))
    return pl.pallas_call(
        _attn_prompt_kernel,
        grid=(BATCH, N_KV_HEADS, nq),
        in_specs=[pl.BlockSpec((MOBA_BLOCK, qw), lambda b, kv, qi: (b * nq + qi, kv)), kv_spec, kv_spec],
        out_specs=pl.BlockSpec((MOBA_BLOCK, qw), lambda b, kv, qi: (b * nq + qi, kv)),
        out_shape=jax.ShapeDtypeStruct((N_TOK, D_ATTN), BF16),
        scratch_shapes=[pltpu.VMEM((SEQ // MOBA_BLOCK, HEAD_DIM), F32),
                        pltpu.VMEM((SEQ, HEAD_DIM), BF16),
                        pltpu.VMEM((SEQ, HEAD_DIM), BF16)],
        compiler_params=_params(("arbitrary", "arbitrary", "arbitrary")),
        name="attn_prompt",
    )(q_rot, k_rot, v)


NEW_ROWS = 8
ROWS_QS = GROUP * DEC_SEQ


def _attn_sample_kernel(pt_ref, q_ref, kn_ref, vn_ref, *rest):
    del pt_ref
    k_pages = rest[:N_PAGES]
    v_pages = rest[N_PAGES:2 * N_PAGES]
    o_ref = rest[2 * N_PAGES]
    nb = PAST_LEN // MOBA_BLOCK
    scale = HEAD_DIM ** -0.5
    lane_blk = lax.broadcasted_iota(jnp.int32, (ROWS_QS, nb), 1)
    all_past = lane_blk < nb
    t_row = lax.broadcasted_iota(jnp.int32, (ROWS_QS, PAGE_SIZE), 0) % DEC_SEQ
    new_ok = (lax.broadcasted_iota(jnp.int32, (ROWS_QS, PAGE_SIZE), 1) <= t_row).astype(jnp.int32)
    zpad = jnp.zeros((PAGE_SIZE - NEW_ROWS, HEAD_DIM), F32)
    for kv in range(N_KV_HEADS):
        q = q_ref[kv]
        k_all = jnp.concatenate([r[:, kv, :] for r in k_pages] + [kn_ref[kv], zpad], axis=0)
        v_all = jnp.concatenate([r[:, kv, :] for r in v_pages] + [vn_ref[kv], zpad], axis=0)
        kmean = jnp.mean(k_all[:PAST_LEN].reshape(nb, MOBA_BLOCK, HEAD_DIM), axis=1)
        gate = lax.dot_general(q, kmean, _NT, precision=lax.Precision.HIGHEST, preferred_element_type=F32)
        sel = _topk_blocks(gate, all_past, lane_blk).astype(jnp.int32)
        s = lax.dot_general(q.astype(BF16), k_all.astype(BF16), _NT, preferred_element_type=F32) * scale
        pieces = [jnp.broadcast_to(sel[:, n:n + 1], (ROWS_QS, MOBA_BLOCK)) for n in range(nb)] + [new_ok]
        s = jnp.where(jnp.concatenate(pieces, axis=1) > 0, s, -jnp.inf)
        m = jnp.max(s, axis=-1, keepdims=True)
        p = jnp.exp(s - m)
        l = jnp.sum(p, axis=-1, keepdims=True)
        o_ref[kv] = jnp.dot(p.astype(BF16), v_all.astype(BF16), preferred_element_type=F32) / l


def _attn_sample(page_table, q_s, k_new, v_new, cache_k, cache_v):
    def page_spec(p):
        return pl.BlockSpec((None, None, PAGE_SIZE, N_KV_HEADS, HEAD_DIM),
                            lambda b, pt, p=p: (0, pt[b * N_PAGES + p], 0, 0, 0))

    q_spec = pl.BlockSpec((None, N_KV_HEADS, ROWS_QS, HEAD_DIM), lambda b, pt: (b, 0, 0, 0))
    n_spec = pl.BlockSpec((None, N_KV_HEADS, NEW_ROWS, HEAD_DIM), lambda b, pt: (b, 0, 0, 0))
    grid_spec = pltpu.PrefetchScalarGridSpec(
        num_scalar_prefetch=1,
        grid=(DEC_BATCH,),
        in_specs=[q_spec, n_spec, n_spec] + [page_spec(p) for p in range(N_PAGES)] * 2,
        out_specs=q_spec,
    )
    return pl.pallas_call(
        _attn_sample_kernel,
        grid_spec=grid_spec,
        out_shape=jax.ShapeDtypeStruct((DEC_BATCH, N_KV_HEADS, ROWS_QS, HEAD_DIM), F32),
        compiler_params=_params(("arbitrary",)),
        name="attn_sample",
    )(page_table.reshape(-1), q_s, k_new, v_new, *([cache_k] * N_PAGES), *([cache_v] * N_PAGES))


def _merge_kernel(a_ref, t_ref, wc_ref, wa_ref, gc_ref, ga_ref, o_ref, wcb_ref, wab_ref):
    @pl.when(pl.program_id(1) == 0)
    def _():
        wcb_ref[...] = wc_ref[...].astype(BF16)
        wab_ref[...] = wa_ref[...].astype(BF16)

    ya = jnp.dot(a_ref[...], wcb_ref[...], preferred_element_type=F32)
    yb = jnp.dot(t_ref[...], wab_ref[...], preferred_element_type=F32)
    o_ref[...] = (_sigmoid(gc_ref[...]) * ya + _sigmoid(ga_ref[...]) * yb).astype(BF16)


def _merge(a_conv, attn, w_conv_branch, w_attn_branch, h):
    act = pl.BlockSpec((TILE, D_CONV), lambda j, i: (i, 0))
    wsp = pl.BlockSpec((D_CONV, TILE), lambda j, i: (0, j))

    def gate(col):
        return pl.BlockSpec((TILE, TILE), lambda j, i, col=col: (i, col // TILE + j))

    return pl.pallas_call(
        _merge_kernel,
        grid=(D_MODEL // TILE, N_TOK // TILE),
        in_specs=[act, act, wsp, wsp, gate(COL_GC), gate(COL_GA)],
        out_specs=pl.BlockSpec((TILE, TILE), lambda j, i: (i, j)),
        out_shape=jax.ShapeDtypeStruct((N_TOK, D_MODEL), BF16),
        scratch_shapes=[pltpu.VMEM((D_CONV, TILE), BF16), pltpu.VMEM((D_ATTN, TILE), BF16)],
        compiler_params=_params(("arbitrary", "arbitrary")),
        name="merge",
    )(a_conv, attn, w_conv_branch, w_attn_branch, h, h)


def _out_proj_kernel(m_ref, w_ref, x_ref, o_ref, wbf_ref):
    @pl.when(pl.program_id(1) == 0)
    def _():
        wbf_ref[...] = w_ref[...].astype(BF16)

    o_ref[...] = ALPHA * x_ref[...] + jnp.dot(m_ref[...], wbf_ref[...], preferred_element_type=F32)


def _out_proj(merged, w_o, x):
    return pl.pallas_call(
        _out_proj_kernel,
        grid=(D_MODEL // TILE, N_TOK // TILE),
        in_specs=[pl.BlockSpec((TILE, D_MODEL), lambda j, i: (i, 0)),
                  pl.BlockSpec((D_MODEL, TILE), lambda j, i: (0, j)),
                  pl.BlockSpec((TILE, TILE), lambda j, i: (i, j))],
        out_specs=pl.BlockSpec((TILE, TILE), lambda j, i: (i, j)),
        out_shape=jax.ShapeDtypeStruct((N_TOK, D_MODEL), F32),
        scratch_shapes=[pltpu.VMEM((D_MODEL, TILE), BF16)],
        compiler_params=_params(("arbitrary", "arbitrary")),
        name="out_proj",
    )(merged, w_o, x)


def _layer_norm(x, g, b):
    mu = jnp.mean(x, axis=-1, keepdims=True)
    xc = x - mu
    var = jnp.mean(xc * xc, axis=-1, keepdims=True)
    return xc * lax.rsqrt(var + LN_EPS) * g + b


def _ln_router_kernel(x_ref, g_ref, b_ref, wr_ref, br_ref, h_ref, e_ref, w_ref):
    h = _layer_norm(x_ref[...], g_ref[...], b_ref[...])
    h_ref[...] = h
    x = jnp.dot(h, wr_ref[...], precision=lax.Precision.HIGHEST, preferred_element_type=F32) + br_ref[...]
    lane = lax.broadcasted_iota(jnp.int32, x.shape, 1)
    lane_f = lane.astype(F32)
    ninf = -jnp.inf

    def first_lane(hit):
        return jnp.min(jnp.where(hit, lane_f, float(LANES)), axis=-1, keepdims=True)

    is_g = lane < N_GROUPS
    glog = jnp.where(is_g, x, ninf)
    gmax = jnp.max(glog, axis=-1, keepdims=True)
    grp = first_lane(glog == gmax)
    wg = 1.0 / jnp.sum(jnp.where(is_g, jnp.exp(x - gmax), 0.0), axis=-1, keepdims=True)
    lane_grp = lax.shift_right_logical(lane, 3).astype(F32)
    in_grp = (lane >= N_GROUPS) & (lane < N_GROUPS + N_EXPERTS) & (lane_grp == grp + 1.0)
    elog = jnp.where(in_grp, x, ninf)
    t1 = jnp.max(elog, axis=-1, keepdims=True)
    i1 = first_lane(elog == t1)
    elog2 = jnp.where(lane_f == i1, ninf, elog)
    t2 = jnp.max(elog2, axis=-1, keepdims=True)
    i2 = first_lane(elog2 == t2)
    e2 = jnp.exp(t2 - t1)
    den = 1.0 + e2
    e_ref[...] = jnp.where(lane == 0, i1 - N_GROUPS, jnp.where(lane == 1, i2 - N_GROUPS, 0.0)).astype(jnp.int32)
    w_ref[...] = jnp.where(lane == 0, wg * (1.0 / den), jnp.where(lane == 1, wg * (e2 / den), 0.0))


def _ln_router(pre, g, b, w_router, b_router):
    tm = MOE_TM
    row = pl.BlockSpec((tm, D_MODEL), lambda i: (i, 0))
    vec = pl.BlockSpec((1, D_MODEL), lambda i: (0, 0))
    small = pl.BlockSpec((tm, LANES), lambda i: (i, 0))
    return pl.pallas_call(
        _ln_router_kernel,
        grid=(N_TOK // tm,),
        in_specs=[row, vec, vec,
                  pl.BlockSpec((D_MODEL, LANES), lambda i: (0, 0)),
                  pl.BlockSpec((1, LANES), lambda i: (0, 0))],
        out_specs=[row, small, small],
        out_shape=[jax.ShapeDtypeStruct((N_TOK, D_MODEL), F32),
                   jax.ShapeDtypeStruct((N_TOK, LANES), jnp.int32),
                   jax.ShapeDtypeStruct((N_TOK, LANES), F32)],
        compiler_params=_params(("arbitrary",)),
        name="ln_router",
    )(pre, g, b, w_router, b_router)


def _row_copy(src_hbm, row, dst_ref, dst_row, sem):
    return pltpu.make_async_copy(src_hbm.at[pl.ds(row, 1)], dst_ref.at[pl.ds(dst_row, 1)], sem)


def _dispatch_kernel(nused_ref, tok_ref, h_hbm, o_ref, buf_ref, sem):
    i = pl.program_id(0)

    @pl.when(i < nused_ref[0])
    def _():
        def issue(r, c):
            _row_copy(h_hbm, tok_ref[0, r], buf_ref, r, sem).start()
            return c

        lax.fori_loop(0, MOE_TM, issue, 0)

        def wait(r, c):
            _row_copy(h_hbm, 0, buf_ref, r, sem).wait()
            return c

        lax.fori_loop(0, MOE_TM, wait, 0)
        o_ref[...] = buf_ref[...].astype(BF16)

    @pl.when(i >= nused_ref[0])
    def _():
        o_ref[...] = jnp.zeros_like(o_ref)


def _dispatch(n_used, row_tok, h1):
    grid_spec = pltpu.PrefetchScalarGridSpec(
        num_scalar_prefetch=1,
        grid=(MOE_NB,),
        in_specs=[pl.BlockSpec((None, 1, MOE_TM), lambda i, nu: (i, 0, 0), memory_space=pltpu.SMEM),
                  pl.BlockSpec(memory_space=pl.ANY)],
        out_specs=pl.BlockSpec((MOE_TM, D_MODEL), lambda i, nu: (i, 0)),
        scratch_shapes=[pltpu.VMEM((MOE_TM, D_MODEL), F32), pltpu.SemaphoreType.DMA(())],
    )
    return pl.pallas_call(
        _dispatch_kernel,
        grid_spec=grid_spec,
        out_shape=jax.ShapeDtypeStruct((MOE_P, D_MODEL), BF16),
        compiler_params=_params(("arbitrary",)),
        name="moe_dispatch",
    )(n_used, row_tok.reshape(MOE_NB, 1, MOE_TM), h1)


def _mlp_up_kernel(blk_ref, f_ref, e_ref, valid_ref, new_ref, x_ref, wg_ref, wu_ref, o_ref, wgb_ref, wub_ref):
    del blk_ref, f_ref, e_ref
    t = pl.program_id(0)

    @pl.when(new_ref[t] == 1)
    def _():
        wgb_ref[...] = wg_ref[...].astype(BF16)
        wub_ref[...] = wu_ref[...].astype(BF16)

    @pl.when(valid_ref[t] == 1)
    def _():
        x = x_ref[...]
        g = jnp.dot(x, wgb_ref[...], preferred_element_type=F32)
        u = jnp.dot(x, wub_ref[...], preferred_element_type=F32)
        o_ref[...] = (g * _sigmoid(g) * u).astype(BF16)


def _mlp_up(items, xs, w_gate, w_up):
    n_items = MOE_NB * (D_EXPERT // MOE_TF)
    wsp = pl.BlockSpec((None, D_MODEL, MOE_TF), lambda t, blk, f, e, v, n: (e[t], 0, f[t]))
    grid_spec = pltpu.PrefetchScalarGridSpec(
        num_scalar_prefetch=5,
        grid=(n_items,),
        in_specs=[pl.BlockSpec((MOE_TM, D_MODEL), lambda t, blk, f, e, v, n: (blk[t], 0)), wsp, wsp],
        out_specs=pl.BlockSpec((MOE_TM, MOE_TF), lambda t, blk, f, e, v, n: (blk[t], f[t])),
        scratch_shapes=[pltpu.VMEM((D_MODEL, MOE_TF), BF16), pltpu.VMEM((D_MODEL, MOE_TF), BF16)],
    )
    return pl.pallas_call(
        _mlp_up_kernel,
        grid_spec=grid_spec,
        out_shape=jax.ShapeDtypeStruct((MOE_P, D_EXPERT), BF16),
        compiler_params=_params(("arbitrary",)),
        name="moe_up",
    )(*items, xs, w_gate, w_up)


def _mlp_down_kernel(blk_ref, f_ref, e_ref, valid_ref, new_ref, x_ref, wd_ref, o_ref, wdb_ref):
    del blk_ref, f_ref, e_ref
    t = pl.program_id(0)

    @pl.when(new_ref[t] == 1)
    def _():
        wdb_ref[...] = wd_ref[...].astype(BF16)

    @pl.when(valid_ref[t] == 1)
    def _():
        o_ref[...] = jnp.dot(x_ref[...], wdb_ref[...], preferred_element_type=F32)


def _mlp_down(items, hmid, w_down):
    n_items = MOE_NB * (D_MODEL // MOE_TN)
    grid_spec = pltpu.PrefetchScalarGridSpec(
        num_scalar_prefetch=5,
        grid=(n_items,),
        in_specs=[pl.BlockSpec((MOE_TM, D_EXPERT), lambda t, blk, f, e, v, n: (blk[t], 0)),
                  pl.BlockSpec((None, D_EXPERT, MOE_TN), lambda t, blk, f, e, v, n: (e[t], 0, f[t]))],
        out_specs=pl.BlockSpec((MOE_TM, MOE_TN), lambda t, blk, f, e, v, n: (blk[t], f[t])),
        scratch_shapes=[pltpu.VMEM((D_EXPERT, MOE_TN), BF16)],
    )
    return pl.pallas_call(
        _mlp_down_kernel,
        grid_spec=grid_spec,
        out_shape=jax.ShapeDtypeStruct((MOE_P, D_MODEL), F32),
        compiler_params=_params(("arbitrary",)),
        name="moe_down",
    )(*items, hmid, w_down)


def _combine_kernel(dest_ref, w_ref, h_ref, g_ref, b_ref, ys_hbm, o_ref, buf_ref, sem):
    def issue(r, c):
        _row_copy(ys_hbm, dest_ref[0, 2 * r], buf_ref.at[0], r, sem).start()
        _row_copy(ys_hbm, dest_ref[0, 2 * r + 1], buf_ref.at[1], r, sem).start()
        return c

    lax.fori_loop(0, MOE_TM, issue, 0)

    def wait(r, c):
        _row_copy(ys_hbm, 0, buf_ref.at[0], r, sem).wait()
        _row_copy(ys_hbm, 0, buf_ref.at[1], r, sem).wait()
        return c

    lax.fori_loop(0, MOE_TM, wait, 0)
    w = w_ref[...]
    moe = buf_ref[0] * w[:, 0:1] + buf_ref[1] * w[:, 1:2]
    o_ref[...] = _layer_norm(ALPHA * h_ref[...] + moe, g_ref[...], b_ref[...])


def _combine(dest, wts, h1, g, b, ys):
    tm = MOE_TM
    row = pl.BlockSpec((tm, D_MODEL), lambda i: (i, 0))
    vec = pl.BlockSpec((1, D_MODEL), lambda i: (0, 0))
    return pl.pallas_call(
        _combine_kernel,
        grid=(N_TOK // tm,),
        in_specs=[pl.BlockSpec((None, 1, TOPK_IN_GROUP * tm), lambda i: (i, 0, 0), memory_space=pltpu.SMEM),
                  pl.BlockSpec((tm, LANES), lambda i: (i, 0)),
                  row, vec, vec,
                  pl.BlockSpec(memory_space=pl.ANY)],
        out_specs=row,
        out_shape=jax.ShapeDtypeStruct((N_TOK, D_MODEL), F32),
        scratch_shapes=[pltpu.VMEM((TOPK_IN_GROUP, tm, D_MODEL), F32), pltpu.SemaphoreType.DMA(())],
        compiler_params=_params(("arbitrary",)),
        name="moe_combine",
    )(dest.reshape(N_TOK // tm, 1, TOPK_IN_GROUP * tm), wts, h1, g, b, ys)


def _moe_tables(eid):
    flat_e = eid.reshape(-1)
    order = jnp.argsort(flat_e)
    se = flat_e[order]
    counts = jnp.zeros((N_EXPERTS,), jnp.int32).at[flat_e].add(1)
    nblk = (counts + MOE_TM - 1) // MOE_TM
    blk_end = jnp.cumsum(nblk)
    blk_start = blk_end - nblk
    start = jnp.cumsum(counts) - counts
    dest_sorted = blk_start[se] * MOE_TM + jnp.arange(N_ASSIGN, dtype=jnp.int32) - start[se]
    row_tok = jnp.zeros((MOE_P,), jnp.int32).at[dest_sorted].set((order // TOPK_IN_GROUP).astype(jnp.int32))
    dest = jnp.zeros((N_ASSIGN,), jnp.int32).at[order].set(dest_sorted)
    n_used = blk_end[-1]

    def items(n_inner):
        n_items = MOE_NB * n_inner
        t = jnp.arange(n_items, dtype=jnp.int32)
        valid = t < n_used * n_inner
        tc = jnp.minimum(t, jnp.maximum(n_used * n_inner - 1, 0))
        e = jnp.minimum(jnp.searchsorted(blk_end * n_inner, tc, side="right"), N_EXPERTS - 1).astype(jnp.int32)
        q = tc - blk_start[e] * n_inner
        nb_e = jnp.maximum(nblk[e], 1)
        f = q // nb_e
        j = q % nb_e
        blk = blk_start[e] + j
        new = (valid & (j == 0)).astype(jnp.int32)
        return (blk.astype(jnp.int32), f.astype(jnp.int32), e, valid.astype(jnp.int32), new)

    return row_tok, dest, n_used.reshape(1).astype(jnp.int32), items


def kernel(x_prompt, x_sample, cache_k, cache_v, state_conv, page_table, w_in, conv_w, w_conv_branch, w_attn_branch, w_o, ln1_g, ln1_b, w_router_group, b_router_group, w_router_expert, b_router_expert, w_gate, w_up, w_down, ln2_g, ln2_b):
    l = 0
    x = jnp.concatenate([x_prompt.reshape(N_P, D_MODEL),
                         x_sample.transpose(1, 0, 2).reshape(N_S, D_MODEL)], axis=0)
    h = _in_proj(x.astype(BF16), w_in[l])

    a_conv, conv_p = _conv_prompt(h, conv_w[l])
    a_s, conv_s_t = _conv_sample(h, state_conv[l].transpose(1, 0, 2), conv_w[l])
    a_conv = lax.dynamic_update_slice(a_conv, a_s, (N_P, 0))

    q_rot, k_rot, v = _rope(h, *_rope_tables())
    attn = _attn_prompt(q_rot, k_rot, v)
    q_s = (q_rot[N_P:].reshape(DEC_SEQ, DEC_BATCH, N_KV_HEADS, GROUP, HEAD_DIM)
           .transpose(1, 2, 3, 0, 4).reshape(DEC_BATCH, N_KV_HEADS, ROWS_QS, HEAD_DIM))

    def new_rows(a):
        a = a[N_P:].reshape(DEC_SEQ, DEC_BATCH, N_KV_HEADS, HEAD_DIM).transpose(1, 2, 0, 3)
        return jnp.pad(a, ((0, 0), (0, 0), (0, NEW_ROWS - DEC_SEQ), (0, 0)))

    o_s = _attn_sample(page_table, q_s, new_rows(k_rot), new_rows(v), cache_k, cache_v)
    o_s = (o_s.reshape(DEC_BATCH, N_KV_HEADS, GROUP, DEC_SEQ, HEAD_DIM)
           .transpose(3, 0, 1, 2, 4).reshape(N_S, D_ATTN).astype(BF16))
    attn = lax.dynamic_update_slice(attn, o_s, (N_P, 0))

    merged = _merge(a_conv, attn, w_conv_branch[l], w_attn_branch[l], h)
    pre = _out_proj(merged, w_o[l], x)

    w_router = jnp.pad(jnp.concatenate([w_router_group[l], w_router_expert[l]], axis=1),
                       ((0, 0), (0, LANES - N_GROUPS - N_EXPERTS)))
    b_router = jnp.pad(jnp.concatenate([b_router_group[l], b_router_expert[l]]),
                       (0, LANES - N_GROUPS - N_EXPERTS)).reshape(1, LANES)
    h1, eid, wts = _ln_router(pre, ln1_g[l].reshape(1, D_MODEL), ln1_b[l].reshape(1, D_MODEL), w_router, b_router)
    row_tok, dest, n_used, items = _moe_tables(eid[:, :TOPK_IN_GROUP])
    xs = _dispatch(n_used, row_tok, h1)
    hmid = _mlp_up(items(D_EXPERT // MOE_TF), xs, w_gate[l], w_up[l])
    ys = _mlp_down(items(D_MODEL // MOE_TN), hmid, w_down[l])
    z = _combine(dest, wts, h1, ln2_g[l].reshape(1, D_MODEL), ln2_b[l].reshape(1, D_MODEL), ys)

    y_prompt = z[:N_P].reshape(BATCH, SEQ, D_MODEL)
    y_sample = z[N_P:].reshape(DEC_SEQ, DEC_BATCH, D_MODEL).transpose(1, 0, 2)
    k_prompt = k_rot[:N_P].reshape(1, BATCH, SEQ, N_KV_HEADS, HEAD_DIM)
    v_prompt = v[:N_P].reshape(1, BATCH, SEQ, N_KV_HEADS, HEAD_DIM)
    k_sample = k_rot[N_P:].reshape(DEC_SEQ, DEC_BATCH, N_KV_HEADS, HEAD_DIM).transpose(1, 0, 2, 3)[None]
    v_sample = v[N_P:].reshape(DEC_SEQ, DEC_BATCH, N_KV_HEADS, HEAD_DIM).transpose(1, 0, 2, 3)[None]
    conv_prompt = conv_p[None]
    conv_sample = conv_s_t.transpose(1, 0, 2)[None]
    return (y_prompt, y_sample, k_prompt, v_prompt, conv_prompt, k_sample, v_sample, conv_sample)
```

```python
import jax
import jax.numpy as jnp
from jax import lax
from jax.experimental import pallas as pl
from jax.experimental.pallas import tpu as pltpu

D_MODEL = 4096
BATCH = 4
SEQ = 2048
DEC_BATCH = 128
DEC_SEQ = 4
PAST_LEN = 2048
PAGE_SIZE = 128
N_PAGES = PAST_LEN // PAGE_SIZE
D_CONV = D_MODEL // 2
CONV_WIDTH = 3
HEAD_DIM = 128
N_HEADS = 16
N_KV_HEADS = 4
GROUP = N_HEADS // N_KV_HEADS
D_ATTN = N_HEADS * HEAD_DIM
D_KV = N_KV_HEADS * HEAD_DIM
ROT_DIM = HEAD_DIM // 4
ROPE_THETA = 500000.0
MOBA_BLOCK = 256
MOBA_TOPK = 3
N_GROUPS = 8
EXPERTS_PER_GROUP = 8
N_EXPERTS = N_GROUPS * EXPERTS_PER_GROUP
TOPK_IN_GROUP = 2
D_EXPERT = D_MODEL // 4
DEPTH = 1
ALPHA = (2 * DEPTH) ** 0.25
LN_EPS = 1e-5
D_IN_TOTAL = 3 * D_CONV + D_ATTN + 2 * D_KV + 2 * D_MODEL

N_P = BATCH * SEQ
N_S = DEC_BATCH * DEC_SEQ
N_TOK = N_P + N_S
N_ASSIGN = N_TOK * TOPK_IN_GROUP

COL_CB, COL_CC, COL_CX = 0, D_CONV, 2 * D_CONV
COL_Q = 3 * D_CONV
COL_K = COL_Q + D_ATTN
COL_V = COL_K + D_KV
COL_GC = COL_V + D_KV
COL_GA = COL_GC + D_MODEL

LANES = 128
SUBLANES = 8
TILE = 512
N_PT = N_P // TILE
LN_TM = 256
LN_PT = N_P // LN_TM
MOE_TM = 512
MOE_NB = -(-N_ASSIGN // MOE_TM) + N_EXPERTS
MOE_P = MOE_NB * MOE_TM
MOE_TF = 512
MOE_TN = 2048
ROW_CHUNKS = D_MODEL // LANES
ROW_PITCH = 40
VMEM_LIMIT = 56 * 1024 * 1024

BF16 = jnp.bfloat16
F32 = jnp.float32
_NT = (((1,), (1,)), ((), ()))
_TN = (((0,), (0,)), ((), ()))


def _params(sem, vmem=VMEM_LIMIT):
    return pltpu.CompilerParams(dimension_semantics=sem, vmem_limit_bytes=vmem)


def _sigmoid(x):
    return 1.0 / (1.0 + jnp.exp(-x))


def _in_proj_kernel(x_ref, w_ref, o_ref, wbf_ref):
    @pl.when(pl.program_id(1) == 0)
    def _():
        wbf_ref[...] = w_ref[...].astype(BF16)

    o_ref[...] = jnp.dot(x_ref[...], wbf_ref[...], preferred_element_type=F32)


def _in_proj(x_bf, w_in):
    return pl.pallas_call(
        _in_proj_kernel,
        grid=(D_IN_TOTAL // TILE, N_TOK // TILE),
        in_specs=[pl.BlockSpec((TILE, D_MODEL), lambda j, i: (i, 0)),
                  pl.BlockSpec((D_MODEL, TILE), lambda j, i: (0, j))],
        out_specs=pl.BlockSpec((TILE, TILE), lambda j, i: (i, j)),
        out_shape=jax.ShapeDtypeStruct((N_TOK, D_IN_TOTAL), F32),
        scratch_shapes=[pltpu.VMEM((D_MODEL, TILE), BF16)],
        compiler_params=_params(("arbitrary", "arbitrary")),
        name="in_proj",
    )(x_bf, w_in)


def _conv_prompt_kernel(cb_ref, cc_ref, cx_ref, w_ref, a_ref, tail_ref, prev_ref):
    @pl.when(pl.program_id(2) == 0)
    def _():
        prev_ref[...] = jnp.zeros_like(prev_ref)

    u = cc_ref[...] * cx_ref[...]
    rows = lax.broadcasted_iota(jnp.int32, u.shape, 0)
    p0 = prev_ref[0:1, :]
    p1 = prev_ref[1:2, :]
    um1 = jnp.where(rows == 0, p1, pltpu.roll(u, 1, 0))
    um2 = jnp.where(rows == 0, p0, jnp.where(rows == 1, p1, pltpu.roll(u, 2, 0)))
    y = w_ref[0:1, :] * um2 + w_ref[1:2, :] * um1 + w_ref[2:3, :] * u
    a_ref[...] = (cb_ref[...] * y).astype(BF16)
    last = u[TILE - 2:TILE, :]
    prev_ref[0:2, :] = last
    tail_ref[...] = last


def _conv_prompt(h, conv_w):
    nr = SEQ // TILE
    nc = D_CONV // TILE

    def sec(col):
        return pl.BlockSpec((TILE, TILE), lambda b, c, r, col=col: (b * nr + r, col // TILE + c))

    return pl.pallas_call(
        _conv_prompt_kernel,
        grid=(BATCH, nc, nr),
        in_specs=[sec(COL_CB), sec(COL_CC), sec(COL_CX),
                  pl.BlockSpec((CONV_WIDTH, TILE), lambda b, c, r: (0, c))],
        out_specs=[pl.BlockSpec((TILE, TILE), lambda b, c, r: (b * nr + r, c)),
                   pl.BlockSpec((None, CONV_WIDTH - 1, TILE), lambda b, c, r: (b, 0, c))],
        out_shape=[jax.ShapeDtypeStruct((N_P, D_CONV), BF16),
                   jax.ShapeDtypeStruct((BATCH, CONV_WIDTH - 1, D_CONV), F32)],
        scratch_shapes=[pltpu.VMEM((SUBLANES, TILE), F32)],
        compiler_params=_params(("arbitrary", "arbitrary", "arbitrary")),
        name="conv_prompt",
    )(h, h, h, conv_w)


def _conv_sample_kernel(cb_ref, cc_ref, cx_ref, st_ref, w_ref, a_ref, tail_ref):
    b = DEC_BATCH
    w0, w1, w2 = w_ref[0:1, :], w_ref[1:2, :], w_ref[2:3, :]
    u = cc_ref[...] * cx_ref[...]
    up = [st_ref[0], st_ref[1]] + [u[t * b:(t + 1) * b, :] for t in range(DEC_SEQ)]
    for t in range(DEC_SEQ):
        y = w0 * up[t] + w1 * up[t + 1] + w2 * up[t + 2]
        a_ref[t * b:(t + 1) * b, :] = (cb_ref[t * b:(t + 1) * b, :] * y).astype(BF16)
    tail_ref[0] = up[DEC_SEQ]
    tail_ref[1] = up[DEC_SEQ + 1]


def _conv_sample(h, state_t, conv_w):
    nc = D_CONV // TILE

    def sec(col):
        return pl.BlockSpec((N_S, TILE), lambda c, col=col: (N_PT, col // TILE + c))

    return pl.pallas_call(
        _conv_sample_kernel,
        grid=(nc,),
        in_specs=[sec(COL_CB), sec(COL_CC), sec(COL_CX),
                  pl.BlockSpec((CONV_WIDTH - 1, DEC_BATCH, TILE), lambda c: (0, 0, c)),
                  pl.BlockSpec((CONV_WIDTH, TILE), lambda c: (0, c))],
        out_specs=[pl.BlockSpec((N_S, TILE), lambda c: (0, c)),
                   pl.BlockSpec((CONV_WIDTH - 1, DEC_BATCH, TILE), lambda c: (0, 0, c))],
        out_shape=[jax.ShapeDtypeStruct((N_S, D_CONV), BF16),
                   jax.ShapeDtypeStruct((CONV_WIDTH - 1, DEC_BATCH, D_CONV), F32)],
        compiler_params=_params(("arbitrary",)),
        name="conv_sample",
    )(h, h, h, state_t, conv_w)


def _rope_kernel(q_ref, k_ref, v_ref, c_ref, s1_ref, s2_ref, qp_ref, kp_ref, vp_ref, qs_ref, ks_ref, vs_ref):
    c, s1, s2 = c_ref[...], s1_ref[...], s2_ref[...]

    def rot(x):
        return x * c + pltpu.roll(x, LANES - ROT_DIM // 2, 1) * s1 + pltpu.roll(x, ROT_DIM // 2, 1) * s2

    def emit(qo_ref, ko_ref, vo_ref):
        for hd in range(N_HEADS):
            sl = slice(hd * HEAD_DIM, (hd + 1) * HEAD_DIM)
            qo_ref[:, sl] = rot(q_ref[:, sl])
        for hd in range(N_KV_HEADS):
            sl = slice(hd * HEAD_DIM, (hd + 1) * HEAD_DIM)
            ko_ref[:, sl] = rot(k_ref[:, sl])
        vo_ref[...] = v_ref[...]

    @pl.when(pl.program_id(0) < N_PT)
    def _():
        emit(qp_ref, kp_ref, vp_ref)

    @pl.when(pl.program_id(0) == N_PT)
    def _():
        emit(qs_ref, ks_ref, vs_ref)


def _split_specs(tm, width, n_prompt_tiles):
    return (pl.BlockSpec((tm, width), lambda i: (jnp.minimum(i, n_prompt_tiles - 1), 0)),
            pl.BlockSpec((tm, width), lambda i: (jnp.maximum(i - n_prompt_tiles, 0), 0)))


def _rope(h, tab_c, tab_s1, tab_s2):
    tiles_per_seq = SEQ // TILE

    def tab_map(i):
        return (jnp.where(i < N_PT, i % tiles_per_seq, tiles_per_seq), 0)

    tab = pl.BlockSpec((TILE, LANES), tab_map)
    qp, qs = _split_specs(TILE, D_ATTN, N_PT)
    kp, ks = _split_specs(TILE, D_KV, N_PT)
    return pl.pallas_call(
        _rope_kernel,
        grid=(N_TOK // TILE,),
        in_specs=[pl.BlockSpec((TILE, D_ATTN), lambda i: (i, COL_Q // D_ATTN)),
                  pl.BlockSpec((TILE, D_KV), lambda i: (i, COL_K // D_KV)),
                  pl.BlockSpec((TILE, D_KV), lambda i: (i, COL_V // D_KV)),
                  tab, tab, tab],
        out_specs=[qp, kp, kp, qs, ks, ks],
        out_shape=[jax.ShapeDtypeStruct((N_P, D_ATTN), F32),
                   jax.ShapeDtypeStruct((N_P, D_KV), F32),
                   jax.ShapeDtypeStruct((N_P, D_KV), F32),
                   jax.ShapeDtypeStruct((N_S, D_ATTN), F32),
                   jax.ShapeDtypeStruct((N_S, D_KV), F32),
                   jax.ShapeDtypeStruct((N_S, D_KV), F32)],
        compiler_params=_params(("arbitrary",)),
        name="rope",
    )(h, h, h, tab_c, tab_s1, tab_s2)


def _rope_tables():
    half = ROT_DIM // 2
    inv = ROPE_THETA ** (-jnp.arange(half, dtype=F32) / half)
    pos = jnp.concatenate([jnp.arange(SEQ, dtype=jnp.int32),
                           PAST_LEN + jnp.repeat(jnp.arange(DEC_SEQ, dtype=jnp.int32), DEC_BATCH)])
    ang = pos.astype(F32)[:, None] * inv[None, :]
    cos, sin = jnp.cos(ang), jnp.sin(ang)
    n = pos.shape[0]
    ones = jnp.ones((n, HEAD_DIM - ROT_DIM), F32)
    zeros = jnp.zeros((n, HEAD_DIM - ROT_DIM), F32)
    zh = jnp.zeros((n, half), F32)
    tab_c = jnp.concatenate([cos, cos, ones], 1)
    tab_s1 = jnp.concatenate([-sin, zh, zeros], 1)
    tab_s2 = jnp.concatenate([zh, sin, zeros], 1)
    return tab_c, tab_s1, tab_s2


ROWS_QP = GROUP * MOBA_BLOCK


def _attn_prompt_kernel(q_ref, k_ref, v_ref, o_ref, kmean_ref, kbf_ref, vbf_ref):
    nb = SEQ // MOBA_BLOCK
    qi = pl.program_id(2)

    @pl.when(qi == 0)
    def _():
        k = k_ref[...]
        kmean_ref[...] = jnp.mean(k.reshape(nb, MOBA_BLOCK, HEAD_DIM), axis=1)
        kbf_ref[...] = k.astype(BF16)
        vbf_ref[...] = v_ref[...].astype(BF16)

    q = jnp.concatenate([q_ref[:, g * HEAD_DIM:(g + 1) * HEAD_DIM] for g in range(GROUP)], axis=0)
    gate = lax.dot_general(kmean_ref[...], q, _NT, precision=lax.Precision.HIGHEST, preferred_element_type=F32)
    blk = lax.broadcasted_iota(jnp.int32, gate.shape, 0)
    past = blk < qi
    gate = jnp.where(past, gate, -jnp.inf)
    rank = jnp.zeros(gate.shape, jnp.int32)
    for m in range(nb):
        gm = gate[m:m + 1, :]
        beats = (gm > gate) | ((gm == gate) & (blk > m))
        rank = rank + beats.astype(jnp.int32)
    chosen_t = ((rank < MOBA_TOPK) & past).astype(F32)
    sel = jnp.transpose(jnp.concatenate([chosen_t, jnp.zeros((LANES - nb, ROWS_QP), F32)], axis=0))
    lane = lax.broadcasted_iota(jnp.int32, sel.shape, 1)

    qs = (q * (HEAD_DIM ** -0.5)).astype(BF16)
    own = pl.multiple_of(qi * MOBA_BLOCK, MOBA_BLOCK)
    s = lax.dot_general(qs, kbf_ref[pl.ds(own, MOBA_BLOCK), :], _NT, preferred_element_type=F32)
    q_off = lax.broadcasted_iota(jnp.int32, s.shape, 0) & (MOBA_BLOCK - 1)
    s = jnp.where(lax.broadcasted_iota(jnp.int32, s.shape, 1) <= q_off, s, -jnp.inf)
    m0 = jnp.max(s, axis=-1, keepdims=True)
    p = jnp.exp(s - m0)
    l0 = jnp.sum(p, axis=-1, keepdims=True)
    acc0 = jnp.dot(p.astype(BF16), vbf_ref[pl.ds(own, MOBA_BLOCK), :], preferred_element_type=F32)

    def body(n, carry):
        m, l, acc = carry
        off = pl.multiple_of(n * MOBA_BLOCK, MOBA_BLOCK)
        s = lax.dot_general(qs, kbf_ref[pl.ds(off, MOBA_BLOCK), :], _NT, preferred_element_type=F32)
        chosen = jnp.max(jnp.where(lane == n, sel, 0.0), axis=-1, keepdims=True) > 0.0
        m_new = jnp.where(chosen, jnp.maximum(m, jnp.max(s, axis=-1, keepdims=True)), m)
        p = jnp.where(chosen, jnp.exp(s - m_new), 0.0)
        a = jnp.exp(m - m_new)
        l = a * l + jnp.sum(p, axis=-1, keepdims=True)
        acc = a * acc + jnp.dot(p.astype(BF16), vbf_ref[pl.ds(off, MOBA_BLOCK), :], preferred_element_type=F32)
        return m_new, l, acc

    _, l, acc = lax.fori_loop(0, qi, body, (m0, l0, acc0))
    o = (acc / l).astype(BF16)
    for g in range(GROUP):
        o_ref[:, g * HEAD_DIM:(g + 1) * HEAD_DIM] = o[g * MOBA_BLOCK:(g + 1) * MOBA_BLOCK, :]


def _attn_prompt(q_rot, k_rot, v):
    nq = SEQ // MOBA_BLOCK
    qw = GROUP * HEAD_DIM
    kv_spec = pl.BlockSpec((SEQ, HEAD_DIM), lambda b, kv, qi: (b, kv))
    return pl.pallas_call(
        _attn_prompt_kernel,
        grid=(BATCH, N_KV_HEADS, nq),
        in_specs=[pl.BlockSpec((MOBA_BLOCK, qw), lambda b, kv, qi: (b * nq + qi, kv)), kv_spec, kv_spec],
        out_specs=pl.BlockSpec((MOBA_BLOCK, qw), lambda b, kv, qi: (b * nq + qi, kv)),
        out_shape=jax.ShapeDtypeStruct((N_P, D_ATTN), BF16),
        scratch_shapes=[pltpu.VMEM((SEQ // MOBA_BLOCK, HEAD_DIM), F32),
                        pltpu.VMEM((SEQ, HEAD_DIM), BF16),
                        pltpu.VMEM((SEQ, HEAD_DIM), BF16)],
        compiler_params=_params(("arbitrary", "arbitrary", "arbitrary")),
        name="attn_prompt",
    )(q_rot, k_rot, v)


PAGE_ROWS = PAGE_SIZE * N_KV_HEADS
PAST_ROWS = N_PAGES * PAGE_ROWS
NEW_ROWS = DEC_SEQ * N_KV_HEADS
KEY_ROWS = PAST_ROWS + LANES
ROWS_QS = N_KV_HEADS * GROUP * DEC_SEQ
BLOCK_ROWS = MOBA_BLOCK * N_KV_HEADS


def _page_copies(cache_hbm, pt_ref, b, buf_ref, slot, sem):
    return [pltpu.make_async_copy(cache_hbm.at[pl.ds(pl.multiple_of(pt_ref[b * N_PAGES + p] * PAGE_ROWS, PAGE_ROWS),
                                                     PAGE_ROWS)],
                                  buf_ref.at[slot, pl.ds(p * PAGE_ROWS, PAGE_ROWS)], sem.at[slot])
            for p in range(N_PAGES)]


def _attn_sample_kernel(pt_ref, q_ref, kn_ref, vn_ref, ck_hbm, cv_hbm, o_ref, kbuf, vbuf, bias_ref, ksem, vsem):
    b = pl.program_id(0)
    nb = PAST_LEN // MOBA_BLOCK
    slot = b % 2

    @pl.when(b == 0)
    def _():
        for c in _page_copies(ck_hbm, pt_ref, 0, kbuf, 0, ksem) + _page_copies(cv_hbm, pt_ref, 0, vbuf, 0, vsem):
            c.start()
        zeros = jnp.zeros((KEY_ROWS - PAST_ROWS, HEAD_DIM), F32)
        for s in range(2):
            kbuf[s, PAST_ROWS:, :] = zeros
            vbuf[s, PAST_ROWS:, :] = zeros
        key = lax.broadcasted_iota(jnp.int32, (KEY_ROWS, ROWS_QS), 0)
        qrow = lax.broadcasted_iota(jnp.int32, (KEY_ROWS, ROWS_QS), 1)
        same_head = (key & (N_KV_HEADS - 1)) == lax.shift_right_logical(qrow, 4)
        new_t = lax.shift_right_arithmetic(key - PAST_ROWS, 2)
        ok = same_head & ((key < PAST_ROWS) | (new_t <= (qrow & (DEC_SEQ - 1))))
        bias_ref[...] = jnp.where(ok, 0.0, -jnp.inf)

    @pl.when(b + 1 < DEC_BATCH)
    def _():
        nxt = 1 - slot
        for c in (_page_copies(ck_hbm, pt_ref, b + 1, kbuf, nxt, ksem)
                  + _page_copies(cv_hbm, pt_ref, b + 1, vbuf, nxt, vsem)):
            c.start()

    for c in _page_copies(ck_hbm, pt_ref, b, kbuf, slot, ksem) + _page_copies(cv_hbm, pt_ref, b, vbuf, slot, vsem):
        c.wait()
    kbuf[slot, PAST_ROWS:PAST_ROWS + NEW_ROWS, :] = kn_ref[...]
    vbuf[slot, PAST_ROWS:PAST_ROWS + NEW_ROWS, :] = vn_ref[...]

    q = q_ref[...]
    k_all = kbuf[slot]
    ksum = jnp.sum(k_all[:PAST_ROWS].reshape(nb, BLOCK_ROWS // SUBLANES, SUBLANES, HEAD_DIM), axis=1)
    ksum = (ksum + pltpu.roll(ksum, N_KV_HEADS, 1)).reshape(nb * SUBLANES, HEAD_DIM)
    gate = lax.dot_general(ksum, q, _NT, precision=lax.Precision.HIGHEST, preferred_element_type=F32)
    grow = lax.broadcasted_iota(jnp.int32, gate.shape, 0)
    gq = lax.broadcasted_iota(jnp.int32, gate.shape, 1)
    mine = (grow & (SUBLANES - 1)) == lax.shift_right_logical(gq, 4)
    gblk = lax.shift_right_logical(grow, 3)
    gate = jnp.where(mine, gate, -jnp.inf)
    rank = jnp.zeros(gate.shape, jnp.int32)
    for n in range(nb):
        gn = jnp.max(jnp.where(gblk == n, gate, -jnp.inf), axis=0, keepdims=True)
        beats = (gn > gate) | ((gn == gate) & (gblk > n))
        rank = rank + beats.astype(jnp.int32)
    sel = mine & (rank < MOBA_TOPK)
    blk_bias = [jnp.where(jnp.max(jnp.where((gblk == n) & sel, 1.0, 0.0), axis=0, keepdims=True) > 0.0, 0.0, -jnp.inf)
                for n in range(nb)]

    qs = (q * (HEAD_DIM ** -0.5)).astype(BF16)
    s = lax.dot_general(k_all.astype(BF16), qs, _NT, preferred_element_type=F32) + bias_ref[...]
    parts = [s[n * BLOCK_ROWS:(n + 1) * BLOCK_ROWS] + blk_bias[n] for n in range(nb)] + [s[PAST_ROWS:]]
    s = jnp.concatenate(parts, axis=0)
    m = jnp.max(s, axis=0, keepdims=True)
    p = jnp.exp(s - m)
    l = jnp.sum(p, axis=0, keepdims=True)
    p = (p * (1.0 / l)).astype(BF16)
    o_ref[...] = lax.dot_general(p, vbuf[slot].astype(BF16), _TN, preferred_element_type=F32)


def _attn_sample(page_table, q_s, k_new, v_new, cache_k, cache_v):
    q_spec = pl.BlockSpec((None, ROWS_QS, HEAD_DIM), lambda b, pt: (b, 0, 0))
    n_spec = pl.BlockSpec((None, NEW_ROWS, HEAD_DIM), lambda b, pt: (b, 0, 0))
    any_spec = pl.BlockSpec(memory_space=pl.ANY)
    grid_spec = pltpu.PrefetchScalarGridSpec(
        num_scalar_prefetch=1,
        grid=(DEC_BATCH,),
        in_specs=[q_spec, n_spec, n_spec, any_spec, any_spec],
        out_specs=q_spec,
        scratch_shapes=[pltpu.VMEM((2, KEY_ROWS, HEAD_DIM), F32),
                        pltpu.VMEM((2, KEY_ROWS, HEAD_DIM), F32),
                        pltpu.VMEM((KEY_ROWS, ROWS_QS), F32),
                        pltpu.SemaphoreType.DMA((2,)),
                        pltpu.SemaphoreType.DMA((2,))],
    )
    return pl.pallas_call(
        _attn_sample_kernel,
        grid_spec=grid_spec,
        out_shape=jax.ShapeDtypeStruct((DEC_BATCH, ROWS_QS, HEAD_DIM), F32),
        compiler_params=_params(("arbitrary",)),
        name="attn_sample",
    )(page_table.reshape(-1), q_s, k_new, v_new, cache_k, cache_v)


def _merge_kernel(ap_ref, as_ref, tp_ref, ts_ref, wc_ref, wa_ref, gc_ref, ga_ref, o_ref, wcb_ref, wab_ref):
    i = pl.program_id(1)

    @pl.when(i == 0)
    def _():
        wcb_ref[...] = wc_ref[...].astype(BF16)
        wab_ref[...] = wa_ref[...].astype(BF16)

    def emit(a_ref, t_ref):
        ya = jnp.dot(a_ref[...], wcb_ref[...], preferred_element_type=F32)
        yb = jnp.dot(t_ref[...], wab_ref[...], preferred_element_type=F32)
        o_ref[...] = (_sigmoid(gc_ref[...]) * ya + _sigmoid(ga_ref[...]) * yb).astype(BF16)

    @pl.when(i < N_PT)
    def _():
        emit(ap_ref, tp_ref)

    @pl.when(i == N_PT)
    def _():
        emit(as_ref, ts_ref)


def _merge(a_p, a_s, attn_p, attn_s, w_conv_branch, w_attn_branch, h):
    act_p = pl.BlockSpec((TILE, D_CONV), lambda j, i: (jnp.minimum(i, N_PT - 1), 0))
    act_s = pl.BlockSpec((TILE, D_CONV), lambda j, i: (0, 0))
    wsp = pl.BlockSpec((D_CONV, TILE), lambda j, i: (0, j))

    def gate(col):
        return pl.BlockSpec((TILE, TILE), lambda j, i, col=col: (i, col // TILE + j))

    return pl.pallas_call(
        _merge_kernel,
        grid=(D_MODEL // TILE, N_TOK // TILE),
        in_specs=[act_p, act_s, act_p, act_s, wsp, wsp, gate(COL_GC), gate(COL_GA)],
        out_specs=pl.BlockSpec((TILE, TILE), lambda j, i: (i, j)),
        out_shape=jax.ShapeDtypeStruct((N_TOK, D_MODEL), BF16),
        scratch_shapes=[pltpu.VMEM((D_CONV, TILE), BF16), pltpu.VMEM((D_ATTN, TILE), BF16)],
        compiler_params=_params(("arbitrary", "arbitrary")),
        name="merge",
    )(a_p, a_s, attn_p, attn_s, w_conv_branch, w_attn_branch, h, h)


def _out_proj_kernel(m_ref, w_ref, xp_ref, xs_ref, o_ref, wbf_ref):
    i = pl.program_id(1)

    @pl.when(i == 0)
    def _():
        wbf_ref[...] = w_ref[...].astype(BF16)

    y = jnp.dot(m_ref[...], wbf_ref[...], preferred_element_type=F32)

    @pl.when(i < N_PT)
    def _():
        o_ref[...] = ALPHA * xp_ref[...] + y

    @pl.when(i == N_PT)
    def _():
        o_ref[...] = ALPHA * xs_ref[...] + y


def _out_proj(merged, w_o, x_p, x_s):
    return pl.pallas_call(
        _out_proj_kernel,
        grid=(D_MODEL // TILE, N_TOK // TILE),
        in_specs=[pl.BlockSpec((TILE, D_MODEL), lambda j, i: (i, 0)),
                  pl.BlockSpec((D_MODEL, TILE), lambda j, i: (0, j)),
                  pl.BlockSpec((TILE, TILE), lambda j, i: (jnp.minimum(i, N_PT - 1), j)),
                  pl.BlockSpec((TILE, TILE), lambda j, i: (0, j))],
        out_specs=pl.BlockSpec((TILE, TILE), lambda j, i: (i, j)),
        out_shape=jax.ShapeDtypeStruct((N_TOK, D_MODEL), F32),
        scratch_shapes=[pltpu.VMEM((D_MODEL, TILE), BF16)],
        compiler_params=_params(("arbitrary", "arbitrary")),
        name="out_proj",
    )(merged, w_o, x_p, x_s)


def _layer_norm(x, g, b):
    mu = jnp.mean(x, axis=-1, keepdims=True)
    xc = x - mu
    var = jnp.mean(xc * xc, axis=-1, keepdims=True)
    return xc * lax.rsqrt(var + LN_EPS) * g + b


def _rows_to_chunks(x, dst_ref, n_rows, first_chunk=0):
    for c in range(x.shape[1] // LANES):
        dst_ref[pl.ds(first_chunk + c, n_rows, stride=ROW_PITCH), :] = x[:, c * LANES:(c + 1) * LANES]


def _pad_chunks(dst_ref, n_rows):
    zeros = jnp.zeros((n_rows, LANES), F32)
    for c in range(ROW_CHUNKS, ROW_PITCH):
        dst_ref[pl.ds(c, n_rows, stride=ROW_PITCH), :] = zeros


def _chunks_to_rows(src_ref, n_rows):
    return [src_ref[pl.ds(c, n_rows, stride=ROW_PITCH), :] for c in range(ROW_CHUNKS)]


def _row_gather_start(idx_ref, n_rows, src_hbm, buf_ref, sem, idx_of, first_row=0):
    def issue(r, c):
        src = pl.multiple_of(idx_ref[0, idx_of(r)] * ROW_PITCH, SUBLANES)
        dst = pl.multiple_of((first_row + r) * ROW_PITCH, SUBLANES)
        pltpu.make_async_copy(src_hbm.at[pl.ds(src, ROW_CHUNKS)], buf_ref.at[pl.ds(dst, ROW_CHUNKS)], sem).start()
        return c

    lax.fori_loop(0, n_rows, issue, 0, unroll=8)


def _row_gather_wait(n_rows, src_hbm, buf_ref, sem):
    n = n_rows * ROW_CHUNKS
    pltpu.make_async_copy(src_hbm.at[pl.ds(0, n)], buf_ref.at[pl.ds(0, n)], sem).wait()


def _ln_router_kernel(x_ref, g_ref, b_ref, wr_ref, br_ref, h_ref, hc_ref, e_ref, w_ref):
    h = _layer_norm(x_ref[...], g_ref[...], b_ref[...])
    h_ref[...] = h
    _rows_to_chunks(h, hc_ref, LN_TM)
    _pad_chunks(hc_ref, LN_TM)
    x = jnp.dot(h, wr_ref[...], precision=lax.Precision.HIGHEST, preferred_element_type=F32) + br_ref[...]
    lane = lax.broadcasted_iota(jnp.int32, x.shape, 1)
    lane_f = lane.astype(F32)
    ninf = -jnp.inf

    def first_lane(hit):
        return jnp.min(jnp.where(hit, lane_f, float(LANES)), axis=-1, keepdims=True)

    is_g = lane < N_GROUPS
    glog = jnp.where(is_g, x, ninf)
    gmax = jnp.max(glog, axis=-1, keepdims=True)
    grp = first_lane(glog == gmax)
    wg = 1.0 / jnp.sum(jnp.where(is_g, jnp.exp(x - gmax), 0.0), axis=-1, keepdims=True)
    lane_grp = lax.shift_right_logical(lane, 3).astype(F32)
    in_grp = (lane >= N_GROUPS) & (lane < N_GROUPS + N_EXPERTS) & (lane_grp == grp + 1.0)
    elog = jnp.where(in_grp, x, ninf)
    t1 = jnp.max(elog, axis=-1, keepdims=True)
    i1 = first_lane(elog == t1)
    elog2 = jnp.where(lane_f == i1, ninf, elog)
    t2 = jnp.max(elog2, axis=-1, keepdims=True)
    i2 = first_lane(elog2 == t2)
    e2 = jnp.exp(t2 - t1)
    den = 1.0 + e2
    e_ref[...] = jnp.where(lane == 0, i1 - N_GROUPS, jnp.where(lane == 1, i2 - N_GROUPS, 0.0)).astype(jnp.int32)
    w_ref[...] = jnp.where(lane == 0, wg * (1.0 / den), jnp.where(lane == 1, wg * (e2 / den), 0.0))


def _ln_router(pre, g, b, w_router, b_router):
    tm = LN_TM
    row = pl.BlockSpec((tm, D_MODEL), lambda i: (i, 0))
    vec = pl.BlockSpec((1, D_MODEL), lambda i: (0, 0))
    small = pl.BlockSpec((tm, LANES), lambda i: (i, 0))
    return pl.pallas_call(
        _ln_router_kernel,
        grid=(N_TOK // tm,),
        in_specs=[row, vec, vec,
                  pl.BlockSpec((D_MODEL, LANES), lambda i: (0, 0)),
                  pl.BlockSpec((1, LANES), lambda i: (0, 0))],
        out_specs=[row, pl.BlockSpec((tm * ROW_PITCH, LANES), lambda i: (i, 0)), small, small],
        out_shape=[jax.ShapeDtypeStruct((N_TOK, D_MODEL), F32),
                   jax.ShapeDtypeStruct((N_TOK * ROW_PITCH, LANES), F32),
                   jax.ShapeDtypeStruct((N_TOK, LANES), jnp.int32),
                   jax.ShapeDtypeStruct((N_TOK, LANES), F32)],
        compiler_params=_params(("arbitrary",)),
        name="ln_router",
    )(pre, g, b, w_router, b_router)


def _dispatch_kernel(nused_ref, tok_ref, hc_hbm, o_ref, buf_ref, sem):
    @pl.when(pl.program_id(0) < nused_ref[0])
    def _():
        _row_gather_start(tok_ref, MOE_TM, hc_hbm, buf_ref, sem, lambda r: r)
        _row_gather_wait(MOE_TM, hc_hbm, buf_ref, sem)
        for c, chunk in enumerate(_chunks_to_rows(buf_ref, MOE_TM)):
            o_ref[:, c * LANES:(c + 1) * LANES] = chunk.astype(BF16)


def _dispatch(n_used, row_tok, h_chunks):
    grid_spec = pltpu.PrefetchScalarGridSpec(
        num_scalar_prefetch=1,
        grid=(MOE_NB,),
        in_specs=[pl.BlockSpec((None, 1, MOE_TM), lambda i, nu: (i, 0, 0), memory_space=pltpu.SMEM),
                  pl.BlockSpec(memory_space=pl.ANY)],
        out_specs=pl.BlockSpec((MOE_TM, D_MODEL), lambda i, nu: (jnp.minimum(i, nu[0] - 1), 0)),
        scratch_shapes=[pltpu.VMEM((MOE_TM * ROW_PITCH, LANES), F32), pltpu.SemaphoreType.DMA(())],
    )
    return pl.pallas_call(
        _dispatch_kernel,
        grid_spec=grid_spec,
        out_shape=jax.ShapeDtypeStruct((MOE_P, D_MODEL), BF16),
        compiler_params=_params(("arbitrary",)),
        name="moe_dispatch",
    )(n_used, row_tok.reshape(MOE_NB, 1, MOE_TM), h_chunks)


def _mlp_up_kernel(blk_ref, f_ref, e_ref, valid_ref, new_ref, x_ref, wg_ref, wu_ref, o_ref, wgb_ref, wub_ref):
    del blk_ref, f_ref, e_ref
    t = pl.program_id(0)

    @pl.when(new_ref[t] == 1)
    def _():
        wgb_ref[...] = wg_ref[...].astype(BF16)
        wub_ref[...] = wu_ref[...].astype(BF16)

    @pl.when(valid_ref[t] == 1)
    def _():
        x = x_ref[...]
        g = jnp.dot(x, wgb_ref[...], preferred_element_type=F32)
        u = jnp.dot(x, wub_ref[...], preferred_element_type=F32)
        o_ref[...] = (g * _sigmoid(g) * u).astype(BF16)


def _mlp_up(items, xs, w_gate, w_up):
    n_items = MOE_NB * (D_EXPERT // MOE_TF)
    wsp = pl.BlockSpec((None, D_MODEL, MOE_TF), lambda t, blk, f, e, v, n: (e[t], 0, f[t]))
    grid_spec = pltpu.PrefetchScalarGridSpec(
        num_scalar_prefetch=5,
        grid=(n_items,),
        in_specs=[pl.BlockSpec((MOE_TM, D_MODEL), lambda t, blk, f, e, v, n: (blk[t], 0)), wsp, wsp],
        out_specs=pl.BlockSpec((MOE_TM, MOE_TF), lambda t, blk, f, e, v, n: (blk[t], f[t])),
        scratch_shapes=[pltpu.VMEM((D_MODEL, MOE_TF), BF16), pltpu.VMEM((D_MODEL, MOE_TF), BF16)],
    )
    return pl.pallas_call(
        _mlp_up_kernel,
        grid_spec=grid_spec,
        out_shape=jax.ShapeDtypeStruct((MOE_P, D_EXPERT), BF16),
        compiler_params=_params(("arbitrary",)),
        name="moe_up",
    )(*items, xs, w_gate, w_up)


def _mlp_down_kernel(blk_ref, f_ref, e_ref, valid_ref, new_ref, x_ref, wd_ref, o_ref, wdb_ref):
    del blk_ref, e_ref
    t = pl.program_id(0)

    @pl.when(new_ref[t] == 1)
    def _():
        wdb_ref[...] = wd_ref[...].astype(BF16)

    @pl.when(valid_ref[t] == 1)
    def _():
        y = jnp.dot(x_ref[...], wdb_ref[...], preferred_element_type=F32)
        for half in range(D_MODEL // MOE_TN):
            @pl.when(f_ref[t] == half)
            def _(half=half):
                _rows_to_chunks(y, o_ref, MOE_TM, first_chunk=half * (MOE_TN // LANES))

        @pl.when(f_ref[t] == 0)
        def _():
            _pad_chunks(o_ref, MOE_TM)


def _mlp_down(items, hmid, w_down):
    n_items = MOE_NB * (D_MODEL // MOE_TN)
    grid_spec = pltpu.PrefetchScalarGridSpec(
        num_scalar_prefetch=5,
        grid=(n_items,),
        in_specs=[pl.BlockSpec((MOE_TM, D_EXPERT), lambda t, blk, f, e, v, n: (blk[t], 0)),
                  pl.BlockSpec((None, D_EXPERT, MOE_TN), lambda t, blk, f, e, v, n: (e[t], 0, f[t]))],
        out_specs=pl.BlockSpec((MOE_TM * ROW_PITCH, LANES), lambda t, blk, f, e, v, n: (blk[t], 0)),
        scratch_shapes=[pltpu.VMEM((D_EXPERT, MOE_TN), BF16)],
    )
    return pl.pallas_call(
        _mlp_down_kernel,
        grid_spec=grid_spec,
        out_shape=jax.ShapeDtypeStruct((MOE_P * ROW_PITCH, LANES), F32),
        compiler_params=_params(("arbitrary",)),
        name="moe_down",
    )(*items, hmid, w_down)


def _combine_kernel(dest_ref, w_ref, h_ref, g_ref, b_ref, ys_hbm, zp_ref, zs_ref, buf_ref, sem):
    i = pl.program_id(0)
    for k in range(TOPK_IN_GROUP):
        _row_gather_start(dest_ref, LN_TM, ys_hbm, buf_ref, sem, lambda r, k=k: TOPK_IN_GROUP * r + k, first_row=k * LN_TM)
    _row_gather_wait(TOPK_IN_GROUP * LN_TM, ys_hbm, buf_ref, sem)
    w = w_ref[...]
    w0, w1 = w[:, 0:1], w[:, 1:2]
    moe = jnp.concatenate(
        [buf_ref[pl.ds(c, LN_TM, stride=ROW_PITCH), :] * w0
         + buf_ref[pl.ds(LN_TM * ROW_PITCH + c, LN_TM, stride=ROW_PITCH), :] * w1 for c in range(ROW_CHUNKS)], axis=1)
    z = _layer_norm(ALPHA * h_ref[...] + moe, g_ref[...], b_ref[...])

    @pl.when(i < LN_PT)
    def _():
        zp_ref[...] = z

    @pl.when(i >= LN_PT)
    def _():
        zs_ref[...] = z


def _combine(dest, wts, h1, g, b, ys_chunks):
    tm = LN_TM
    row = pl.BlockSpec((tm, D_MODEL), lambda i: (i, 0))
    vec = pl.BlockSpec((1, D_MODEL), lambda i: (0, 0))
    zp, zs = _split_specs(tm, D_MODEL, LN_PT)
    return pl.pallas_call(
        _combine_kernel,
        grid=(N_TOK // tm,),
        in_specs=[pl.BlockSpec((None, 1, TOPK_IN_GROUP * tm), lambda i: (i, 0, 0), memory_space=pltpu.SMEM),
                  pl.BlockSpec((tm, LANES), lambda i: (i, 0)),
                  row, vec, vec,
                  pl.BlockSpec(memory_space=pl.ANY)],
        out_specs=[zp, zs],
        out_shape=[jax.ShapeDtypeStruct((N_P, D_MODEL), F32), jax.ShapeDtypeStruct((N_S, D_MODEL), F32)],
        scratch_shapes=[pltpu.VMEM((TOPK_IN_GROUP * tm * ROW_PITCH, LANES), F32), pltpu.SemaphoreType.DMA(())],
        compiler_params=_params(("arbitrary",)),
        name="moe_combine",
    )(dest.reshape(N_TOK // tm, 1, TOPK_IN_GROUP * tm), wts, h1, g, b, ys_chunks)


def _moe_tables(eid):
    flat_e = eid.reshape(-1)
    experts = jnp.arange(N_EXPERTS, dtype=jnp.int32)
    iota_a = jnp.arange(N_ASSIGN, dtype=jnp.int32)
    se, order = lax.sort((flat_e, iota_a), num_keys=1)
    counts = jnp.sum((flat_e[:, None] == experts[None, :]).astype(jnp.int32), axis=0)
    nblk = (counts + MOE_TM - 1) // MOE_TM
    blk_end = jnp.cumsum(nblk)
    blk_start = blk_end - nblk
    start = jnp.cumsum(counts) - counts
    n_used = blk_end[-1]
    dest_sorted = blk_start[se] * MOE_TM + iota_a - start[se]
    _, dest = lax.sort((order, dest_sorted), num_keys=1)
    rows = jnp.arange(MOE_P, dtype=jnp.int32)
    rblk = rows // MOE_TM
    r_e = jnp.minimum(jnp.sum((blk_end[None, :] <= rblk[:, None]).astype(jnp.int32), axis=1), N_EXPERTS - 1)
    local = rows - blk_start[r_e] * MOE_TM
    r_ok = (rblk < n_used) & (local < counts[r_e])
    row_tok = jnp.where(r_ok, order[jnp.clip(start[r_e] + local, 0, N_ASSIGN - 1)] // TOPK_IN_GROUP, 0)

    def items(n_inner, block_major):
        n_items = MOE_NB * n_inner
        t = jnp.arange(n_items, dtype=jnp.int32)
        valid = t < n_used * n_inner
        tc = jnp.minimum(t, n_used * n_inner - 1)
        e = jnp.minimum(jnp.sum((blk_end[None, :] * n_inner <= tc[:, None]).astype(jnp.int32), axis=1), N_EXPERTS - 1)
        q = tc - blk_start[e] * n_inner
        nb_e = jnp.maximum(nblk[e], 1)
        if block_major:
            j, f = q // n_inner, q % n_inner
        else:
            f, j = q // nb_e, q % nb_e
        blk = blk_start[e] + j
        prev_e = jnp.concatenate([jnp.full((1,), -1, jnp.int32), e[:-1]])
        prev_f = jnp.concatenate([jnp.full((1,), -1, jnp.int32), f[:-1]])
        new = valid & ((e != prev_e) | (f != prev_f))
        return (blk.astype(jnp.int32), f.astype(jnp.int32), e.astype(jnp.int32),
                valid.astype(jnp.int32), new.astype(jnp.int32))

    return row_tok.astype(jnp.int32), dest.astype(jnp.int32), n_used.reshape(1).astype(jnp.int32), items


def kernel(x_prompt, x_sample, cache_k, cache_v, state_conv, page_table, w_in, conv_w, w_conv_branch, w_attn_branch, w_o, ln1_g, ln1_b, w_router_group, b_router_group, w_router_expert, b_router_expert, w_gate, w_up, w_down, ln2_g, ln2_b):
    l = 0
    x_p = x_prompt.reshape(N_P, D_MODEL)
    x_s = x_sample.transpose(1, 0, 2).reshape(N_S, D_MODEL)
    h = _in_proj(jnp.concatenate([x_p, x_s], axis=0).astype(BF16), w_in[l])

    a_p, conv_p = _conv_prompt(h, conv_w[l])
    a_s, conv_s_t = _conv_sample(h, state_conv[l].transpose(1, 0, 2), conv_w[l])

    q_p, k_p, v_p, q_s, k_s, v_s = _rope(h, *_rope_tables())
    attn_p = _attn_prompt(q_p, k_p, v_p)

    def seq_major(a, width):
        return a.reshape(DEC_SEQ, DEC_BATCH, width // HEAD_DIM, HEAD_DIM).transpose(1, 0, 2, 3)

    k_s_b, v_s_b = seq_major(k_s, D_KV), seq_major(v_s, D_KV)
    q_s_b = seq_major(q_s, D_ATTN).transpose(0, 2, 1, 3).reshape(DEC_BATCH, ROWS_QS, HEAD_DIM)
    pool_rows = cache_k.shape[1] * PAGE_ROWS
    o_s = _attn_sample(page_table, q_s_b,
                       k_s_b.reshape(DEC_BATCH, NEW_ROWS, HEAD_DIM), v_s_b.reshape(DEC_BATCH, NEW_ROWS, HEAD_DIM),
                       cache_k[l].reshape(pool_rows, HEAD_DIM), cache_v[l].reshape(pool_rows, HEAD_DIM))
    attn_s = (o_s.reshape(DEC_BATCH, N_HEADS, DEC_SEQ, HEAD_DIM).transpose(2, 0, 1, 3)
              .reshape(N_S, D_ATTN).astype(BF16))

    merged = _merge(a_p, a_s, attn_p, attn_s, w_conv_branch[l], w_attn_branch[l], h)
    pre = _out_proj(merged, w_o[l], x_p, x_s)

    w_router = jnp.pad(jnp.concatenate([w_router_group[l], w_router_expert[l]], axis=1),
                       ((0, 0), (0, LANES - N_GROUPS - N_EXPERTS)))
    b_router = jnp.pad(jnp.concatenate([b_router_group[l], b_router_expert[l]]),
                       (0, LANES - N_GROUPS - N_EXPERTS)).reshape(1, LANES)
    h1, h_chunks, eid, wts = _ln_router(pre, ln1_g[l].reshape(1, D_MODEL), ln1_b[l].reshape(1, D_MODEL),
                                        w_router, b_router)
    row_tok, dest, n_used, items = _moe_tables(eid[:, :TOPK_IN_GROUP])
    xs = _dispatch(n_used, row_tok, h_chunks)
    hmid = _mlp_up(items(D_EXPERT // MOE_TF, False), xs, w_gate[l], w_up[l])
    ys_chunks = _mlp_down(items(D_MODEL // MOE_TN, True), hmid, w_down[l])
    z_p, z_s = _combine(dest, wts, h1, ln2_g[l].reshape(1, D_MODEL), ln2_b[l].reshape(1, D_MODEL), ys_chunks)

    y_prompt = z_p.reshape(BATCH, SEQ, D_MODEL)
    y_sample = z_s.reshape(DEC_SEQ, DEC_BATCH, D_MODEL).transpose(1, 0, 2)
    k_prompt = k_p.reshape(1, BATCH, SEQ, N_KV_HEADS, HEAD_DIM)
    v_prompt = v_p.reshape(1, BATCH, SEQ, N_KV_HEADS, HEAD_DIM)
    conv_prompt = conv_p[None]
    conv_sample = conv_s_t.transpose(1, 0, 2)[None]
    return (y_prompt, y_sample, k_prompt, v_prompt, conv_prompt, k_s_b[None], v_s_b[None], conv_sample)
```

```python
import jax
import jax.numpy as jnp
from jax import lax
from jax.experimental import pallas as pl
from jax.experimental.pallas import tpu as pltpu

D_MODEL = 4096
BATCH = 4
SEQ = 2048
DEC_BATCH = 128
DEC_SEQ = 4
PAST_LEN = 2048
PAGE_SIZE = 128
N_PAGES = PAST_LEN // PAGE_SIZE
D_CONV = D_MODEL // 2
CONV_WIDTH = 3
HEAD_DIM = 128
N_HEADS = 16
N_KV_HEADS = 4
GROUP = N_HEADS // N_KV_HEADS
D_ATTN = N_HEADS * HEAD_DIM
D_KV = N_KV_HEADS * HEAD_DIM
ROT_DIM = HEAD_DIM // 4
ROPE_THETA = 500000.0
MOBA_BLOCK = 256
MOBA_TOPK = 3
N_GROUPS = 8
EXPERTS_PER_GROUP = 8
N_EXPERTS = N_GROUPS * EXPERTS_PER_GROUP
TOPK_IN_GROUP = 2
D_EXPERT = D_MODEL // 4
DEPTH = 1
ALPHA = (2 * DEPTH) ** 0.25
LN_EPS = 1e-5
D_IN_TOTAL = 3 * D_CONV + D_ATTN + 2 * D_KV + 2 * D_MODEL

N_P = BATCH * SEQ
N_S = DEC_BATCH * DEC_SEQ
N_TOK = N_P + N_S
N_ASSIGN = N_TOK * TOPK_IN_GROUP

COL_CB, COL_CC, COL_CX = 0, D_CONV, 2 * D_CONV
COL_Q = 3 * D_CONV
COL_K = COL_Q + D_ATTN
COL_V = COL_K + D_KV
COL_GC = COL_V + D_KV
COL_GA = COL_GC + D_MODEL

LANES = 128
SUBLANES = 8
TILE = 512
N_PT = N_P // TILE
LN_TM = 256
LN_PT = N_P // LN_TM
MOE_TM = 512
MOE_NB = -(-N_ASSIGN // MOE_TM) + N_EXPERTS
MOE_P = MOE_NB * MOE_TM
MOE_TF = 512
MOE_TN = 2048
ROW_CHUNKS = D_MODEL // LANES
ROW_PITCH = 40
VMEM_LIMIT = 56 * 1024 * 1024

BF16 = jnp.bfloat16
F32 = jnp.float32
_NT = (((1,), (1,)), ((), ()))
_TN = (((0,), (0,)), ((), ()))


def _params(sem, vmem=VMEM_LIMIT):
    return pltpu.CompilerParams(dimension_semantics=sem, vmem_limit_bytes=vmem)


def _sigmoid(x):
    return 1.0 / (1.0 + jnp.exp(-x))


def _in_proj_kernel(xp_ref, xs_ref, w_ref, o_ref, wbf_ref):
    i = pl.program_id(1)

    @pl.when(i == 0)
    def _():
        wbf_ref[...] = w_ref[...].astype(BF16)

    @pl.when(i < N_PT)
    def _():
        o_ref[...] = jnp.dot(xp_ref[...], wbf_ref[...], preferred_element_type=F32)

    @pl.when(i == N_PT)
    def _():
        o_ref[...] = jnp.dot(xs_ref[...], wbf_ref[...], preferred_element_type=F32)


def _in_proj(xp_bf, xs_bf, w_in):
    return pl.pallas_call(
        _in_proj_kernel,
        grid=(D_IN_TOTAL // TILE, N_TOK // TILE),
        in_specs=[pl.BlockSpec((TILE, D_MODEL), lambda j, i: (jnp.minimum(i, N_PT - 1), 0)),
                  pl.BlockSpec((TILE, D_MODEL), lambda j, i: (0, 0)),
                  pl.BlockSpec((D_MODEL, TILE), lambda j, i: (0, j))],
        out_specs=pl.BlockSpec((TILE, TILE), lambda j, i: (i, j)),
        out_shape=jax.ShapeDtypeStruct((N_TOK, D_IN_TOTAL), F32),
        scratch_shapes=[pltpu.VMEM((D_MODEL, TILE), BF16)],
        compiler_params=_params(("arbitrary", "arbitrary")),
        name="in_proj",
    )(xp_bf, xs_bf, w_in)


def _conv_prompt_kernel(cb_ref, cc_ref, cx_ref, w_ref, a_ref, tail_ref, prev_ref):
    @pl.when(pl.program_id(2) == 0)
    def _():
        prev_ref[...] = jnp.zeros_like(prev_ref)

    u = cc_ref[...] * cx_ref[...]
    rows = lax.broadcasted_iota(jnp.int32, u.shape, 0)
    p0 = prev_ref[0:1, :]
    p1 = prev_ref[1:2, :]
    um1 = jnp.where(rows == 0, p1, pltpu.roll(u, 1, 0))
    um2 = jnp.where(rows == 0, p0, jnp.where(rows == 1, p1, pltpu.roll(u, 2, 0)))
    y = w_ref[0:1, :] * um2 + w_ref[1:2, :] * um1 + w_ref[2:3, :] * u
    a_ref[...] = (cb_ref[...] * y).astype(BF16)
    last = u[TILE - 2:TILE, :]
    prev_ref[0:2, :] = last
    tail_ref[...] = last


def _conv_prompt(h, conv_w):
    nr = SEQ // TILE
    nc = D_CONV // TILE

    def sec(col):
        return pl.BlockSpec((TILE, TILE), lambda b, c, r, col=col: (b * nr + r, col // TILE + c))

    return pl.pallas_call(
        _conv_prompt_kernel,
        grid=(BATCH, nc, nr),
        in_specs=[sec(COL_CB), sec(COL_CC), sec(COL_CX),
                  pl.BlockSpec((CONV_WIDTH, TILE), lambda b, c, r: (0, c))],
        out_specs=[pl.BlockSpec((TILE, TILE), lambda b, c, r: (b * nr + r, c)),
                   pl.BlockSpec((None, CONV_WIDTH - 1, TILE), lambda b, c, r: (b, 0, c))],
        out_shape=[jax.ShapeDtypeStruct((N_P, D_CONV), BF16),
                   jax.ShapeDtypeStruct((BATCH, CONV_WIDTH - 1, D_CONV), F32)],
        scratch_shapes=[pltpu.VMEM((SUBLANES, TILE), F32)],
        compiler_params=_params(("arbitrary", "arbitrary", "arbitrary")),
        name="conv_prompt",
    )(h, h, h, conv_w)


def _conv_sample_kernel(cb_ref, cc_ref, cx_ref, st_ref, w_ref, a_ref, tail_ref):
    b = DEC_BATCH
    w0, w1, w2 = w_ref[0:1, :], w_ref[1:2, :], w_ref[2:3, :]
    u = cc_ref[...] * cx_ref[...]
    up = [st_ref[0], st_ref[1]] + [u[t * b:(t + 1) * b, :] for t in range(DEC_SEQ)]
    for t in range(DEC_SEQ):
        y = w0 * up[t] + w1 * up[t + 1] + w2 * up[t + 2]
        a_ref[t * b:(t + 1) * b, :] = (cb_ref[t * b:(t + 1) * b, :] * y).astype(BF16)
    tail_ref[0] = up[DEC_SEQ]
    tail_ref[1] = up[DEC_SEQ + 1]


def _conv_sample(h, state_t, conv_w):
    nc = D_CONV // TILE

    def sec(col):
        return pl.BlockSpec((N_S, TILE), lambda c, col=col: (N_PT, col // TILE + c))

    return pl.pallas_call(
        _conv_sample_kernel,
        grid=(nc,),
        in_specs=[sec(COL_CB), sec(COL_CC), sec(COL_CX),
                  pl.BlockSpec((CONV_WIDTH - 1, DEC_BATCH, TILE), lambda c: (0, 0, c)),
                  pl.BlockSpec((CONV_WIDTH, TILE), lambda c: (0, c))],
        out_specs=[pl.BlockSpec((N_S, TILE), lambda c: (0, c)),
                   pl.BlockSpec((CONV_WIDTH - 1, DEC_BATCH, TILE), lambda c: (0, 0, c))],
        out_shape=[jax.ShapeDtypeStruct((N_S, D_CONV), BF16),
                   jax.ShapeDtypeStruct((CONV_WIDTH - 1, DEC_BATCH, D_CONV), F32)],
        compiler_params=_params(("arbitrary",)),
        name="conv_sample",
    )(h, h, h, state_t, conv_w)


def _rope_kernel(q_ref, k_ref, v_ref, c_ref, s1_ref, s2_ref, qp_ref, kp_ref, vp_ref, qs_ref, ks_ref, vs_ref):
    c, s1, s2 = c_ref[...], s1_ref[...], s2_ref[...]

    def rot(x):
        return x * c + pltpu.roll(x, LANES - ROT_DIM // 2, 1) * s1 + pltpu.roll(x, ROT_DIM // 2, 1) * s2

    def emit(qo_ref, ko_ref, vo_ref):
        for hd in range(N_HEADS):
            sl = slice(hd * HEAD_DIM, (hd + 1) * HEAD_DIM)
            qo_ref[:, sl] = rot(q_ref[:, sl])
        for hd in range(N_KV_HEADS):
            sl = slice(hd * HEAD_DIM, (hd + 1) * HEAD_DIM)
            ko_ref[:, sl] = rot(k_ref[:, sl])
        vo_ref[...] = v_ref[...]

    @pl.when(pl.program_id(0) < N_PT)
    def _():
        emit(qp_ref, kp_ref, vp_ref)

    @pl.when(pl.program_id(0) == N_PT)
    def _():
        emit(qs_ref, ks_ref, vs_ref)


def _split_specs(tm, width, n_prompt_tiles):
    return (pl.BlockSpec((tm, width), lambda i: (jnp.minimum(i, n_prompt_tiles - 1), 0)),
            pl.BlockSpec((tm, width), lambda i: (jnp.maximum(i - n_prompt_tiles, 0), 0)))


def _rope(h, tab_c, tab_s1, tab_s2):
    tiles_per_seq = SEQ // TILE

    def tab_map(i):
        return (jnp.where(i < N_PT, i % tiles_per_seq, tiles_per_seq), 0)

    tab = pl.BlockSpec((TILE, LANES), tab_map)
    qp, qs = _split_specs(TILE, D_ATTN, N_PT)
    kp, ks = _split_specs(TILE, D_KV, N_PT)
    return pl.pallas_call(
        _rope_kernel,
        grid=(N_TOK // TILE,),
        in_specs=[pl.BlockSpec((TILE, D_ATTN), lambda i: (i, COL_Q // D_ATTN)),
                  pl.BlockSpec((TILE, D_KV), lambda i: (i, COL_K // D_KV)),
                  pl.BlockSpec((TILE, D_KV), lambda i: (i, COL_V // D_KV)),
                  tab, tab, tab],
        out_specs=[qp, kp, kp, qs, ks, ks],
        out_shape=[jax.ShapeDtypeStruct((N_P, D_ATTN), F32),
                   jax.ShapeDtypeStruct((N_P, D_KV), F32),
                   jax.ShapeDtypeStruct((N_P, D_KV), F32),
                   jax.ShapeDtypeStruct((N_S, D_ATTN), F32),
                   jax.ShapeDtypeStruct((N_S, D_KV), F32),
                   jax.ShapeDtypeStruct((N_S, D_KV), F32)],
        compiler_params=_params(("arbitrary",)),
        name="rope",
    )(h, h, h, tab_c, tab_s1, tab_s2)


def _rope_tables():
    half = ROT_DIM // 2
    inv = ROPE_THETA ** (-jnp.arange(half, dtype=F32) / half)
    pos = jnp.concatenate([jnp.arange(SEQ, dtype=jnp.int32),
                           PAST_LEN + jnp.repeat(jnp.arange(DEC_SEQ, dtype=jnp.int32), DEC_BATCH)])
    ang = pos.astype(F32)[:, None] * inv[None, :]
    cos, sin = jnp.cos(ang), jnp.sin(ang)
    n = pos.shape[0]
    ones = jnp.ones((n, HEAD_DIM - ROT_DIM), F32)
    zeros = jnp.zeros((n, HEAD_DIM - ROT_DIM), F32)
    zh = jnp.zeros((n, half), F32)
    tab_c = jnp.concatenate([cos, cos, ones], 1)
    tab_s1 = jnp.concatenate([-sin, zh, zeros], 1)
    tab_s2 = jnp.concatenate([zh, sin, zeros], 1)
    return tab_c, tab_s1, tab_s2


ROWS_QP = GROUP * MOBA_BLOCK


def _attn_prompt_kernel(q_ref, k_ref, v_ref, o_ref, kmean_ref, kbf_ref, vbf_ref):
    nb = SEQ // MOBA_BLOCK
    qi = pl.program_id(2)

    @pl.when(qi == 0)
    def _():
        k = k_ref[...]
        kmean_ref[...] = jnp.mean(k.reshape(nb, MOBA_BLOCK, HEAD_DIM), axis=1)
        kbf_ref[...] = k.astype(BF16)
        vbf_ref[...] = v_ref[...].astype(BF16)

    q = jnp.concatenate([q_ref[:, g * HEAD_DIM:(g + 1) * HEAD_DIM] for g in range(GROUP)], axis=0)
    gate = lax.dot_general(kmean_ref[...], q, _NT, precision=lax.Precision.HIGHEST, preferred_element_type=F32)
    blk = lax.broadcasted_iota(jnp.int32, gate.shape, 0)
    past = blk < qi
    gate = jnp.where(past, gate, -jnp.inf)
    rank = jnp.zeros(gate.shape, jnp.int32)
    for m in range(nb):
        gm = gate[m:m + 1, :]
        beats = (gm > gate) | ((gm == gate) & (blk > m))
        rank = rank + beats.astype(jnp.int32)
    chosen_t = ((rank < MOBA_TOPK) & past).astype(F32)
    sel = jnp.transpose(jnp.concatenate([chosen_t, jnp.zeros((LANES - nb, ROWS_QP), F32)], axis=0))
    lane = lax.broadcasted_iota(jnp.int32, sel.shape, 1)

    qs = (q * (HEAD_DIM ** -0.5)).astype(BF16)
    own = pl.multiple_of(qi * MOBA_BLOCK, MOBA_BLOCK)
    s = lax.dot_general(qs, kbf_ref[pl.ds(own, MOBA_BLOCK), :], _NT, preferred_element_type=F32)
    q_off = lax.broadcasted_iota(jnp.int32, s.shape, 0) & (MOBA_BLOCK - 1)
    s = jnp.where(lax.broadcasted_iota(jnp.int32, s.shape, 1) <= q_off, s, -jnp.inf)
    m0 = jnp.max(s, axis=-1, keepdims=True)
    p = jnp.exp(s - m0)
    l0 = jnp.sum(p, axis=-1, keepdims=True)
    acc0 = jnp.dot(p.astype(BF16), vbf_ref[pl.ds(own, MOBA_BLOCK), :], preferred_element_type=F32)

    def body(n, carry):
        m, l, acc = carry
        off = pl.multiple_of(n * MOBA_BLOCK, MOBA_BLOCK)
        s = lax.dot_general(qs, kbf_ref[pl.ds(off, MOBA_BLOCK), :], _NT, preferred_element_type=F32)
        chosen = jnp.max(jnp.where(lane == n, sel, 0.0), axis=-1, keepdims=True) > 0.0
        m_new = jnp.where(chosen, jnp.maximum(m, jnp.max(s, axis=-1, keepdims=True)), m)
        p = jnp.where(chosen, jnp.exp(s - m_new), 0.0)
        a = jnp.exp(m - m_new)
        l = a * l + jnp.sum(p, axis=-1, keepdims=True)
        acc = a * acc + jnp.dot(p.astype(BF16), vbf_ref[pl.ds(off, MOBA_BLOCK), :], preferred_element_type=F32)
        return m_new, l, acc

    _, l, acc = lax.fori_loop(0, qi, body, (m0, l0, acc0))
    o = (acc / l).astype(BF16)
    for g in range(GROUP):
        o_ref[:, g * HEAD_DIM:(g + 1) * HEAD_DIM] = o[g * MOBA_BLOCK:(g + 1) * MOBA_BLOCK, :]


def _attn_prompt(q_rot, k_rot, v):
    nq = SEQ // MOBA_BLOCK
    qw = GROUP * HEAD_DIM
    kv_spec = pl.BlockSpec((SEQ, HEAD_DIM), lambda b, kv, qi: (b, kv))
    return pl.pallas_call(
        _attn_prompt_kernel,
        grid=(BATCH, N_KV_HEADS, nq),
        in_specs=[pl.BlockSpec((MOBA_BLOCK, qw), lambda b, kv, qi: (b * nq + qi, kv)), kv_spec, kv_spec],
        out_specs=pl.BlockSpec((MOBA_BLOCK, qw), lambda b, kv, qi: (b * nq + qi, kv)),
        out_shape=jax.ShapeDtypeStruct((N_P, D_ATTN), BF16),
        scratch_shapes=[pltpu.VMEM((SEQ // MOBA_BLOCK, HEAD_DIM), F32),
                        pltpu.VMEM((SEQ, HEAD_DIM), BF16),
                        pltpu.VMEM((SEQ, HEAD_DIM), BF16)],
        compiler_params=_params(("arbitrary", "arbitrary", "arbitrary")),
        name="attn_prompt",
    )(q_rot, k_rot, v)


PAGE_ROWS = PAGE_SIZE * N_KV_HEADS
PAST_ROWS = N_PAGES * PAGE_ROWS
NEW_ROWS = DEC_SEQ * N_KV_HEADS
KEY_ROWS = PAST_ROWS + LANES
ROWS_QS = N_KV_HEADS * GROUP * DEC_SEQ
BLOCK_ROWS = MOBA_BLOCK * N_KV_HEADS


def _page_copies(cache_hbm, pt_ref, b, buf_ref, slot, sem):
    return [pltpu.make_async_copy(cache_hbm.at[pl.ds(pl.multiple_of(pt_ref[b * N_PAGES + p] * PAGE_ROWS, PAGE_ROWS),
                                                     PAGE_ROWS)],
                                  buf_ref.at[slot, pl.ds(p * PAGE_ROWS, PAGE_ROWS)], sem.at[slot])
            for p in range(N_PAGES)]


def _attn_sample_kernel(pt_ref, q_ref, kn_ref, vn_ref, ck_hbm, cv_hbm, o_ref, kbuf, vbuf, bias_ref, ksem, vsem):
    b = pl.program_id(0)
    nb = PAST_LEN // MOBA_BLOCK
    slot = b % 2

    @pl.when(b == 0)
    def _():
        for c in _page_copies(ck_hbm, pt_ref, 0, kbuf, 0, ksem) + _page_copies(cv_hbm, pt_ref, 0, vbuf, 0, vsem):
            c.start()
        zeros = jnp.zeros((KEY_ROWS - PAST_ROWS, HEAD_DIM), F32)
        for s in range(2):
            kbuf[s, PAST_ROWS:, :] = zeros
            vbuf[s, PAST_ROWS:, :] = zeros
        key = lax.broadcasted_iota(jnp.int32, (KEY_ROWS, ROWS_QS), 0)
        qrow = lax.broadcasted_iota(jnp.int32, (KEY_ROWS, ROWS_QS), 1)
        same_head = (key & (N_KV_HEADS - 1)) == lax.shift_right_logical(qrow, 4)
        new_t = lax.shift_right_arithmetic(key - PAST_ROWS, 2)
        ok = same_head & ((key < PAST_ROWS) | (new_t <= (qrow & (DEC_SEQ - 1))))
        bias_ref[...] = jnp.where(ok, 0.0, -jnp.inf)

    @pl.when(b + 1 < DEC_BATCH)
    def _():
        nxt = 1 - slot
        for c in (_page_copies(ck_hbm, pt_ref, b + 1, kbuf, nxt, ksem)
                  + _page_copies(cv_hbm, pt_ref, b + 1, vbuf, nxt, vsem)):
            c.start()

    for c in _page_copies(ck_hbm, pt_ref, b, kbuf, slot, ksem) + _page_copies(cv_hbm, pt_ref, b, vbuf, slot, vsem):
        c.wait()
    kbuf[slot, PAST_ROWS:PAST_ROWS + NEW_ROWS, :] = kn_ref[...]
    vbuf[slot, PAST_ROWS:PAST_ROWS + NEW_ROWS, :] = vn_ref[...]

    q = q_ref[...]
    k_all = kbuf[slot]
    ksum = jnp.sum(k_all[:PAST_ROWS].reshape(nb, BLOCK_ROWS // SUBLANES, SUBLANES, HEAD_DIM), axis=1)
    ksum = (ksum + pltpu.roll(ksum, N_KV_HEADS, 1)).reshape(nb * SUBLANES, HEAD_DIM)
    gate = lax.dot_general(ksum, q, _NT, precision=lax.Precision.HIGHEST, preferred_element_type=F32)
    grow = lax.broadcasted_iota(jnp.int32, gate.shape, 0)
    gq = lax.broadcasted_iota(jnp.int32, gate.shape, 1)
    mine = (grow & (SUBLANES - 1)) == lax.shift_right_logical(gq, 4)
    gblk = lax.shift_right_logical(grow, 3)
    gate = jnp.where(mine, gate, -jnp.inf)
    rank = jnp.zeros(gate.shape, jnp.int32)
    for n in range(nb):
        gn = jnp.max(jnp.where(gblk == n, gate, -jnp.inf), axis=0, keepdims=True)
        beats = (gn > gate) | ((gn == gate) & (gblk > n))
        rank = rank + beats.astype(jnp.int32)
    sel = mine & (rank < MOBA_TOPK)
    blk_bias = [jnp.where(jnp.max(jnp.where((gblk == n) & sel, 1.0, 0.0), axis=0, keepdims=True) > 0.0, 0.0, -jnp.inf)
                for n in range(nb)]

    qs = (q * (HEAD_DIM ** -0.5)).astype(BF16)
    s = lax.dot_general(k_all.astype(BF16), qs, _NT, preferred_element_type=F32) + bias_ref[...]
    parts = [s[n * BLOCK_ROWS:(n + 1) * BLOCK_ROWS] + blk_bias[n] for n in range(nb)] + [s[PAST_ROWS:]]
    s = jnp.concatenate(parts, axis=0)
    m = jnp.max(s, axis=0, keepdims=True)
    p = jnp.exp(s - m)
    l = jnp.sum(p, axis=0, keepdims=True)
    p = (p * (1.0 / l)).astype(BF16)
    o_ref[...] = lax.dot_general(p, vbuf[slot].astype(BF16), _TN, preferred_element_type=F32)


def _attn_sample(page_table, q_s, k_new, v_new, cache_k, cache_v):
    q_spec = pl.BlockSpec((None, ROWS_QS, HEAD_DIM), lambda b, pt: (b, 0, 0))
    n_spec = pl.BlockSpec((None, NEW_ROWS, HEAD_DIM), lambda b, pt: (b, 0, 0))
    any_spec = pl.BlockSpec(memory_space=pl.ANY)
    grid_spec = pltpu.PrefetchScalarGridSpec(
        num_scalar_prefetch=1,
        grid=(DEC_BATCH,),
        in_specs=[q_spec, n_spec, n_spec, any_spec, any_spec],
        out_specs=q_spec,
        scratch_shapes=[pltpu.VMEM((2, KEY_ROWS, HEAD_DIM), F32),
                        pltpu.VMEM((2, KEY_ROWS, HEAD_DIM), F32),
                        pltpu.VMEM((KEY_ROWS, ROWS_QS), F32),
                        pltpu.SemaphoreType.DMA((2,)),
                        pltpu.SemaphoreType.DMA((2,))],
    )
    return pl.pallas_call(
        _attn_sample_kernel,
        grid_spec=grid_spec,
        out_shape=jax.ShapeDtypeStruct((DEC_BATCH, ROWS_QS, HEAD_DIM), F32),
        compiler_params=_params(("arbitrary",)),
        name="attn_sample",
    )(page_table.reshape(-1), q_s, k_new, v_new, cache_k, cache_v)


def _merge_kernel(ap_ref, as_ref, tp_ref, ts_ref, wc_ref, wa_ref, gc_ref, ga_ref, o_ref, wcb_ref, wab_ref):
    i = pl.program_id(1)

    @pl.when(i == 0)
    def _():
        wcb_ref[...] = wc_ref[...].astype(BF16)
        wab_ref[...] = wa_ref[...].astype(BF16)

    def emit(a_ref, t_ref):
        ya = jnp.dot(a_ref[...], wcb_ref[...], preferred_element_type=F32)
        yb = jnp.dot(t_ref[...], wab_ref[...], preferred_element_type=F32)
        o_ref[...] = (_sigmoid(gc_ref[...]) * ya + _sigmoid(ga_ref[...]) * yb).astype(BF16)

    @pl.when(i < N_PT)
    def _():
        emit(ap_ref, tp_ref)

    @pl.when(i == N_PT)
    def _():
        emit(as_ref, ts_ref)


def _merge(a_p, a_s, attn_p, attn_s, w_conv_branch, w_attn_branch, h):
    act_p = pl.BlockSpec((TILE, D_CONV), lambda j, i: (jnp.minimum(i, N_PT - 1), 0))
    act_s = pl.BlockSpec((TILE, D_CONV), lambda j, i: (0, 0))
    wsp = pl.BlockSpec((D_CONV, TILE), lambda j, i: (0, j))

    def gate(col):
        return pl.BlockSpec((TILE, TILE), lambda j, i, col=col: (i, col // TILE + j))

    return pl.pallas_call(
        _merge_kernel,
        grid=(D_MODEL // TILE, N_TOK // TILE),
        in_specs=[act_p, act_s, act_p, act_s, wsp, wsp, gate(COL_GC), gate(COL_GA)],
        out_specs=pl.BlockSpec((TILE, TILE), lambda j, i: (i, j)),
        out_shape=jax.ShapeDtypeStruct((N_TOK, D_MODEL), BF16),
        scratch_shapes=[pltpu.VMEM((D_CONV, TILE), BF16), pltpu.VMEM((D_ATTN, TILE), BF16)],
        compiler_params=_params(("arbitrary", "arbitrary")),
        name="merge",
    )(a_p, a_s, attn_p, attn_s, w_conv_branch, w_attn_branch, h, h)


def _out_proj_kernel(m_ref, w_ref, xp_ref, xs_ref, o_ref, wbf_ref):
    i = pl.program_id(1)

    @pl.when(i == 0)
    def _():
        wbf_ref[...] = w_ref[...].astype(BF16)

    y = jnp.dot(m_ref[...], wbf_ref[...], preferred_element_type=F32)

    @pl.when(i < N_PT)
    def _():
        o_ref[...] = ALPHA * xp_ref[...] + y

    @pl.when(i == N_PT)
    def _():
        o_ref[...] = ALPHA * xs_ref[...] + y


def _out_proj(merged, w_o, x_p, x_s):
    return pl.pallas_call(
        _out_proj_kernel,
        grid=(D_MODEL // TILE, N_TOK // TILE),
        in_specs=[pl.BlockSpec((TILE, D_MODEL), lambda j, i: (i, 0)),
                  pl.BlockSpec((D_MODEL, TILE), lambda j, i: (0, j)),
                  pl.BlockSpec((TILE, TILE), lambda j, i: (jnp.minimum(i, N_PT - 1), j)),
                  pl.BlockSpec((TILE, TILE), lambda j, i: (0, j))],
        out_specs=pl.BlockSpec((TILE, TILE), lambda j, i: (i, j)),
        out_shape=jax.ShapeDtypeStruct((N_TOK, D_MODEL), F32),
        scratch_shapes=[pltpu.VMEM((D_MODEL, TILE), BF16)],
        compiler_params=_params(("arbitrary", "arbitrary")),
        name="out_proj",
    )(merged, w_o, x_p, x_s)


def _layer_norm(x, g, b):
    mu = jnp.mean(x, axis=-1, keepdims=True)
    xc = x - mu
    var = jnp.mean(xc * xc, axis=-1, keepdims=True)
    return xc * lax.rsqrt(var + LN_EPS) * g + b


def _rows_to_chunks(x, dst_ref, n_rows, first_chunk=0):
    for c in range(x.shape[1] // LANES):
        dst_ref[pl.ds(first_chunk + c, n_rows, stride=ROW_PITCH), :] = x[:, c * LANES:(c + 1) * LANES]


def _pad_chunks(dst_ref, n_rows):
    zeros = jnp.zeros((n_rows, LANES), F32)
    for c in range(ROW_CHUNKS, ROW_PITCH):
        dst_ref[pl.ds(c, n_rows, stride=ROW_PITCH), :] = zeros


def _chunks_to_rows(src_ref, n_rows):
    return [src_ref[pl.ds(c, n_rows, stride=ROW_PITCH), :] for c in range(ROW_CHUNKS)]


GATHER_UNROLL = 8


def _chunk_row_copy(src_hbm, src_row, buf_ref, dst_row, sem):
    src = pl.multiple_of(src_row * ROW_PITCH, SUBLANES)
    dst = pl.multiple_of(dst_row * ROW_PITCH, SUBLANES)
    return pltpu.make_async_copy(src_hbm.at[pl.ds(src, ROW_CHUNKS)], buf_ref.at[pl.ds(dst, ROW_CHUNKS)], sem)


def _ln_router_kernel(x_ref, g_ref, b_ref, wr_ref, br_ref, h_ref, hc_ref, e_ref, w_ref):
    h = _layer_norm(x_ref[...], g_ref[...], b_ref[...])
    h_ref[...] = h
    _rows_to_chunks(h, hc_ref, LN_TM)
    _pad_chunks(hc_ref, LN_TM)
    x = jnp.dot(h, wr_ref[...], precision=lax.Precision.HIGHEST, preferred_element_type=F32) + br_ref[...]
    lane = lax.broadcasted_iota(jnp.int32, x.shape, 1)
    lane_f = lane.astype(F32)
    ninf = -jnp.inf

    def first_lane(hit):
        return jnp.min(jnp.where(hit, lane_f, float(LANES)), axis=-1, keepdims=True)

    is_g = lane < N_GROUPS
    glog = jnp.where(is_g, x, ninf)
    gmax = jnp.max(glog, axis=-1, keepdims=True)
    grp = first_lane(glog == gmax)
    wg = 1.0 / jnp.sum(jnp.where(is_g, jnp.exp(x - gmax), 0.0), axis=-1, keepdims=True)
    lane_grp = lax.shift_right_logical(lane, 3).astype(F32)
    in_grp = (lane >= N_GROUPS) & (lane < N_GROUPS + N_EXPERTS) & (lane_grp == grp + 1.0)
    elog = jnp.where(in_grp, x, ninf)
    t1 = jnp.max(elog, axis=-1, keepdims=True)
    i1 = first_lane(elog == t1)
    elog2 = jnp.where(lane_f == i1, ninf, elog)
    t2 = jnp.max(elog2, axis=-1, keepdims=True)
    i2 = first_lane(elog2 == t2)
    e2 = jnp.exp(t2 - t1)
    den = 1.0 + e2
    e_ref[...] = jnp.where(lane == 0, i1 - N_GROUPS, jnp.where(lane == 1, i2 - N_GROUPS, 0.0)).astype(jnp.int32)
    w_ref[...] = jnp.where(lane == 0, wg * (1.0 / den), jnp.where(lane == 1, wg * (e2 / den), 0.0))


def _ln_router(pre, g, b, w_router, b_router):
    tm = LN_TM
    row = pl.BlockSpec((tm, D_MODEL), lambda i: (i, 0))
    vec = pl.BlockSpec((1, D_MODEL), lambda i: (0, 0))
    small = pl.BlockSpec((tm, LANES), lambda i: (i, 0))
    return pl.pallas_call(
        _ln_router_kernel,
        grid=(N_TOK // tm,),
        in_specs=[row, vec, vec,
                  pl.BlockSpec((D_MODEL, LANES), lambda i: (0, 0)),
                  pl.BlockSpec((1, LANES), lambda i: (0, 0))],
        out_specs=[row, pl.BlockSpec((tm * ROW_PITCH, LANES), lambda i: (i, 0)), small, small],
        out_shape=[jax.ShapeDtypeStruct((N_TOK, D_MODEL), F32),
                   jax.ShapeDtypeStruct((N_TOK * ROW_PITCH, LANES), F32),
                   jax.ShapeDtypeStruct((N_TOK, LANES), jnp.int32),
                   jax.ShapeDtypeStruct((N_TOK, LANES), F32)],
        compiler_params=_params(("arbitrary",)),
        name="ln_router",
    )(pre, g, b, w_router, b_router)


def _dispatch_kernel(nused_ref, src_ref, cnt_ref, tok_ref, hc_hbm, o_ref, buf_ref, sem):
    i = pl.program_id(0)

    @pl.when(i == 0)
    def _():
        buf_ref[...] = jnp.zeros_like(buf_ref)

    @pl.when(i < nused_ref[0])
    def _():
        cnt = cnt_ref[i]
        base = src_ref[i]
        n_groups = (cnt + GATHER_UNROLL - 1) // GATHER_UNROLL

        def issue(g, c):
            for u in range(GATHER_UNROLL):
                r = g * GATHER_UNROLL + u
                tok = tok_ref[base + jnp.minimum(r, cnt - 1)]
                _chunk_row_copy(hc_hbm, tok, buf_ref, r, sem).start(priority=u % 2)
            return c

        lax.fori_loop(0, n_groups, issue, 0)

        def wait(g, c):
            n = GATHER_UNROLL * ROW_CHUNKS
            pltpu.make_async_copy(hc_hbm.at[pl.ds(0, n)], buf_ref.at[pl.ds(0, n)], sem).wait()
            return c

        lax.fori_loop(0, n_groups, wait, 0)
        live = lax.broadcasted_iota(jnp.int32, (MOE_TM, LANES), 0) < cnt
        for c, chunk in enumerate(_chunks_to_rows(buf_ref, MOE_TM)):
            o_ref[:, c * LANES:(c + 1) * LANES] = jnp.where(live, chunk, 0.0).astype(BF16)


def _dispatch(n_used, blk_src, blk_cnt, sorted_tok, h_chunks):
    grid_spec = pltpu.PrefetchScalarGridSpec(
        num_scalar_prefetch=3,
        grid=(MOE_NB,),
        in_specs=[pl.BlockSpec(memory_space=pltpu.SMEM),
                  pl.BlockSpec(memory_space=pl.ANY)],
        out_specs=pl.BlockSpec((MOE_TM, D_MODEL), lambda i, nu, src, cnt: (jnp.minimum(i, nu[0] - 1), 0)),
        scratch_shapes=[pltpu.VMEM((MOE_TM * ROW_PITCH, LANES), F32), pltpu.SemaphoreType.DMA(())],
    )
    return pl.pallas_call(
        _dispatch_kernel,
        grid_spec=grid_spec,
        out_shape=jax.ShapeDtypeStruct((MOE_P, D_MODEL), BF16),
        compiler_params=_params(("arbitrary",)),
        name="moe_dispatch",
    )(n_used, blk_src, blk_cnt, sorted_tok, h_chunks)


def _mlp_up_kernel(blk_ref, f_ref, e_ref, valid_ref, new_ref, x_ref, wg_ref, wu_ref, o_ref, wgb_ref, wub_ref):
    del blk_ref, f_ref, e_ref
    t = pl.program_id(0)

    @pl.when(new_ref[t] == 1)
    def _():
        wgb_ref[...] = wg_ref[...].astype(BF16)
        wub_ref[...] = wu_ref[...].astype(BF16)

    @pl.when(valid_ref[t] == 1)
    def _():
        x = x_ref[...]
        g = jnp.dot(x, wgb_ref[...], preferred_element_type=F32)
        u = jnp.dot(x, wub_ref[...], preferred_element_type=F32)
        o_ref[...] = (g * _sigmoid(g) * u).astype(BF16)


def _mlp_up(items, xs, w_gate, w_up):
    n_items = MOE_NB * (D_EXPERT // MOE_TF)
    wsp = pl.BlockSpec((None, D_MODEL, MOE_TF), lambda t, blk, f, e, v, n: (e[t], 0, f[t]))
    grid_spec = pltpu.PrefetchScalarGridSpec(
        num_scalar_prefetch=5,
        grid=(n_items,),
        in_specs=[pl.BlockSpec((MOE_TM, D_MODEL), lambda t, blk, f, e, v, n: (blk[t], 0)), wsp, wsp],
        out_specs=pl.BlockSpec((MOE_TM, MOE_TF), lambda t, blk, f, e, v, n: (blk[t], f[t])),
        scratch_shapes=[pltpu.VMEM((D_MODEL, MOE_TF), BF16), pltpu.VMEM((D_MODEL, MOE_TF), BF16)],
    )
    return pl.pallas_call(
        _mlp_up_kernel,
        grid_spec=grid_spec,
        out_shape=jax.ShapeDtypeStruct((MOE_P, D_EXPERT), BF16),
        compiler_params=_params(("arbitrary",)),
        name="moe_up",
    )(*items, xs, w_gate, w_up)


def _mlp_down_kernel(blk_ref, f_ref, e_ref, valid_ref, new_ref, x_ref, wd_ref, o_ref, wdb_ref):
    del blk_ref, f_ref, e_ref
    t = pl.program_id(0)

    @pl.when(new_ref[t] == 1)
    def _():
        wdb_ref[...] = wd_ref[...].astype(BF16)

    @pl.when(valid_ref[t] == 1)
    def _():
        o_ref[...] = jnp.dot(x_ref[...], wdb_ref[...], preferred_element_type=F32)


def _mlp_down(items, hmid, w_down):
    n_items = MOE_NB * (D_MODEL // MOE_TN)
    grid_spec = pltpu.PrefetchScalarGridSpec(
        num_scalar_prefetch=5,
        grid=(n_items,),
        in_specs=[pl.BlockSpec((MOE_TM, D_EXPERT), lambda t, blk, f, e, v, n: (blk[t], 0)),
                  pl.BlockSpec((None, D_EXPERT, MOE_TN), lambda t, blk, f, e, v, n: (e[t], 0, f[t]))],
        out_specs=pl.BlockSpec((MOE_TM, MOE_TN), lambda t, blk, f, e, v, n: (blk[t], f[t])),
        scratch_shapes=[pltpu.VMEM((D_EXPERT, MOE_TN), BF16)],
    )
    return pl.pallas_call(
        _mlp_down_kernel,
        grid_spec=grid_spec,
        out_shape=jax.ShapeDtypeStruct((MOE_P, D_MODEL), F32),
        compiler_params=_params(("arbitrary",)),
        name="moe_down",
    )(*items, hmid, w_down)


def _combine_kernel(dest_ref, w_ref, h_ref, g_ref, b_ref, ys_hbm, zp_ref, zs_ref, buf_ref, sem):
    i = pl.program_id(0)

    def row_copy(src_row, k, r):
        return pltpu.make_async_copy(ys_hbm.at[pl.ds(src_row, 1)], buf_ref.at[k, pl.ds(r, 1)], sem)

    def issue(g, c):
        for u in range(GATHER_UNROLL):
            r = g * GATHER_UNROLL + u
            for k in range(TOPK_IN_GROUP):
                row_copy(dest_ref[0, TOPK_IN_GROUP * r + k], k, r).start(priority=(u + k) % 2)
        return c

    lax.fori_loop(0, LN_TM // GATHER_UNROLL, issue, 0)
    for k in range(TOPK_IN_GROUP):
        pltpu.make_async_copy(ys_hbm.at[pl.ds(0, LN_TM)], buf_ref.at[k], sem).wait()
    w = w_ref[...]
    moe = buf_ref[0] * w[:, 0:1] + buf_ref[1] * w[:, 1:2]
    z = _layer_norm(ALPHA * h_ref[...] + moe, g_ref[...], b_ref[...])

    @pl.when(i < LN_PT)
    def _():
        zp_ref[...] = z

    @pl.when(i >= LN_PT)
    def _():
        zs_ref[...] = z


def _combine(dest, wts, h1, g, b, ys):
    tm = LN_TM
    row = pl.BlockSpec((tm, D_MODEL), lambda i: (i, 0))
    vec = pl.BlockSpec((1, D_MODEL), lambda i: (0, 0))
    zp, zs = _split_specs(tm, D_MODEL, LN_PT)
    return pl.pallas_call(
        _combine_kernel,
        grid=(N_TOK // tm,),
        in_specs=[pl.BlockSpec((None, 1, TOPK_IN_GROUP * tm), lambda i: (i, 0, 0), memory_space=pltpu.SMEM),
                  pl.BlockSpec((tm, LANES), lambda i: (i, 0)),
                  row, vec, vec,
                  pl.BlockSpec(memory_space=pl.ANY)],
        out_specs=[zp, zs],
        out_shape=[jax.ShapeDtypeStruct((N_P, D_MODEL), F32), jax.ShapeDtypeStruct((N_S, D_MODEL), F32)],
        scratch_shapes=[pltpu.VMEM((TOPK_IN_GROUP, tm, D_MODEL), F32), pltpu.SemaphoreType.DMA(())],
        compiler_params=_params(("arbitrary",)),
        name="moe_combine",
    )(dest.reshape(N_TOK // tm, 1, TOPK_IN_GROUP * tm), wts, h1, g, b, ys)


def _moe_tables(eid):
    flat_e = eid.reshape(-1)
    experts = jnp.arange(N_EXPERTS, dtype=jnp.int32)
    iota_a = jnp.arange(N_ASSIGN, dtype=jnp.int32)
    se, order = lax.sort((flat_e, iota_a), num_keys=1)
    counts = jnp.sum((flat_e[:, None] == experts[None, :]).astype(jnp.int32), axis=0)
    nblk = (counts + MOE_TM - 1) // MOE_TM
    blk_end = jnp.cumsum(nblk)
    blk_start = blk_end - nblk
    start = jnp.cumsum(counts) - counts
    n_used = blk_end[-1]
    dest_sorted = blk_start[se] * MOE_TM + iota_a - start[se]
    _, dest = lax.sort((order, dest_sorted), num_keys=1)
    blks = jnp.arange(MOE_NB, dtype=jnp.int32)
    b_e = jnp.minimum(jnp.sum((blk_end[None, :] <= blks[:, None]).astype(jnp.int32), axis=1), N_EXPERTS - 1)
    b_first = (blks - blk_start[b_e]) * MOE_TM
    blk_src = jnp.clip(start[b_e] + b_first, 0, N_ASSIGN - 1)
    blk_cnt = jnp.clip(counts[b_e] - b_first, 1, MOE_TM)
    sorted_tok = order // TOPK_IN_GROUP

    def items(n_inner, block_major):
        n_items = MOE_NB * n_inner
        t = jnp.arange(n_items, dtype=jnp.int32)
        valid = t < n_used * n_inner
        tc = jnp.minimum(t, n_used * n_inner - 1)
        e = jnp.minimum(jnp.sum((blk_end[None, :] * n_inner <= tc[:, None]).astype(jnp.int32), axis=1), N_EXPERTS - 1)
        q = tc - blk_start[e] * n_inner
        nb_e = jnp.maximum(nblk[e], 1)
        if block_major:
            j, f = q // n_inner, q % n_inner
        else:
            f, j = q // nb_e, q % nb_e
        blk = blk_start[e] + j
        prev_e = jnp.concatenate([jnp.full((1,), -1, jnp.int32), e[:-1]])
        prev_f = jnp.concatenate([jnp.full((1,), -1, jnp.int32), f[:-1]])
        new = valid & ((e != prev_e) | (f != prev_f))
        return (blk.astype(jnp.int32), f.astype(jnp.int32), e.astype(jnp.int32),
                valid.astype(jnp.int32), new.astype(jnp.int32))

    dispatch_tables = (n_used.reshape(1).astype(jnp.int32), blk_src.astype(jnp.int32), blk_cnt.astype(jnp.int32),
                       sorted_tok.astype(jnp.int32))
    return dispatch_tables, dest.astype(jnp.int32), items


def kernel(x_prompt, x_sample, cache_k, cache_v, state_conv, page_table, w_in, conv_w, w_conv_branch, w_attn_branch, w_o, ln1_g, ln1_b, w_router_group, b_router_group, w_router_expert, b_router_expert, w_gate, w_up, w_down, ln2_g, ln2_b):
    l = 0
    x_p = x_prompt.reshape(N_P, D_MODEL)
    x_s = x_sample.transpose(1, 0, 2).reshape(N_S, D_MODEL)
    h = _in_proj(x_p.astype(BF16), x_s.astype(BF16), w_in[l])

    a_p, conv_p = _conv_prompt(h, conv_w[l])
    a_s, conv_s_t = _conv_sample(h, state_conv[l].transpose(1, 0, 2), conv_w[l])

    q_p, k_p, v_p, q_s, k_s, v_s = _rope(h, *_rope_tables())
    attn_p = _attn_prompt(q_p, k_p, v_p)

    def seq_major(a, width):
        return a.reshape(DEC_SEQ, DEC_BATCH, width // HEAD_DIM, HEAD_DIM).transpose(1, 0, 2, 3)

    k_s_b, v_s_b = seq_major(k_s, D_KV), seq_major(v_s, D_KV)
    q_s_b = seq_major(q_s, D_ATTN).transpose(0, 2, 1, 3).reshape(DEC_BATCH, ROWS_QS, HEAD_DIM)
    pool_rows = cache_k.shape[1] * PAGE_ROWS
    o_s = _attn_sample(page_table, q_s_b,
                       k_s_b.reshape(DEC_BATCH, NEW_ROWS, HEAD_DIM), v_s_b.reshape(DEC_BATCH, NEW_ROWS, HEAD_DIM),
                       cache_k[l].reshape(pool_rows, HEAD_DIM), cache_v[l].reshape(pool_rows, HEAD_DIM))
    attn_s = (o_s.reshape(DEC_BATCH, N_HEADS, DEC_SEQ, HEAD_DIM).transpose(2, 0, 1, 3)
              .reshape(N_S, D_ATTN).astype(BF16))

    merged = _merge(a_p, a_s, attn_p, attn_s, w_conv_branch[l], w_attn_branch[l], h)
    pre = _out_proj(merged, w_o[l], x_p, x_s)

    w_router = jnp.pad(jnp.concatenate([w_router_group[l], w_router_expert[l]], axis=1),
                       ((0, 0), (0, LANES - N_GROUPS - N_EXPERTS)))
    b_router = jnp.pad(jnp.concatenate([b_router_group[l], b_router_expert[l]]),
                       (0, LANES - N_GROUPS - N_EXPERTS)).reshape(1, LANES)
    h1, h_chunks, eid, wts = _ln_router(pre, ln1_g[l].reshape(1, D_MODEL), ln1_b[l].reshape(1, D_MODEL),
                                        w_router, b_router)
    dispatch_tables, dest, items = _moe_tables(eid[:, :TOPK_IN_GROUP])
    xs = _dispatch(*dispatch_tables, h_chunks)
    hmid = _mlp_up(items(D_EXPERT // MOE_TF, False), xs, w_gate[l], w_up[l])
    ys = _mlp_down(items(D_MODEL // MOE_TN, False), hmid, w_down[l])
    z_p, z_s = _combine(dest, wts, h1, ln2_g[l].reshape(1, D_MODEL), ln2_b[l].reshape(1, D_MODEL), ys)

    y_prompt = z_p.reshape(BATCH, SEQ, D_MODEL)
    y_sample = z_s.reshape(DEC_SEQ, DEC_BATCH, D_MODEL).transpose(1, 0, 2)
    k_prompt = k_p.reshape(1, BATCH, SEQ, N_KV_HEADS, HEAD_DIM)
    v_prompt = v_p.reshape(1, BATCH, SEQ, N_KV_HEADS, HEAD_DIM)
    conv_prompt = conv_p[None]
    conv_sample = conv_s_t.transpose(1, 0, 2)[None]
    return (y_prompt, y_sample, k_prompt, v_prompt, conv_prompt, k_s_b[None], v_s_b[None], conv_sample)
```

```python
import jax
import jax.numpy as jnp
from jax import lax
from jax.experimental import pallas as pl
from jax.experimental.pallas import tpu as pltpu

D_MODEL = 4096
BATCH = 4
SEQ = 2048
DEC_BATCH = 128
DEC_SEQ = 4
PAST_LEN = 2048
PAGE_SIZE = 128
N_PAGES = PAST_LEN // PAGE_SIZE
D_CONV = D_MODEL // 2
CONV_WIDTH = 3
HEAD_DIM = 128
N_HEADS = 16
N_KV_HEADS = 4
GROUP = N_HEADS // N_KV_HEADS
D_ATTN = N_HEADS * HEAD_DIM
D_KV = N_KV_HEADS * HEAD_DIM
ROT_DIM = HEAD_DIM // 4
ROPE_THETA = 500000.0
MOBA_BLOCK = 256
MOBA_TOPK = 3
N_GROUPS = 8
EXPERTS_PER_GROUP = 8
N_EXPERTS = N_GROUPS * EXPERTS_PER_GROUP
TOPK_IN_GROUP = 2
D_EXPERT = D_MODEL // 4
DEPTH = 1
ALPHA = (2 * DEPTH) ** 0.25
LN_EPS = 1e-5
D_IN_TOTAL = 3 * D_CONV + D_ATTN + 2 * D_KV + 2 * D_MODEL

N_P = BATCH * SEQ
N_S = DEC_BATCH * DEC_SEQ
N_TOK = N_P + N_S
N_ASSIGN = N_TOK * TOPK_IN_GROUP

COL_CB, COL_CC, COL_CX = 0, D_CONV, 2 * D_CONV
COL_Q = 3 * D_CONV
COL_K = COL_Q + D_ATTN
COL_V = COL_K + D_KV
COL_GC = COL_V + D_KV
COL_GA = COL_GC + D_MODEL

LANES = 128
SUBLANES = 8
TILE = 512
N_PT = N_P // TILE
LN_TM = 256
LN_PT = N_P // LN_TM
MOE_TM = 384
MOE_NB = -(-N_ASSIGN // MOE_TM) + N_EXPERTS
MOE_P = MOE_NB * MOE_TM
MOE_TF = 512
MOE_TN = 2048
ROW_CHUNKS = D_MODEL // LANES
ROW_PITCH = 40
VMEM_LIMIT = 56 * 1024 * 1024

BF16 = jnp.bfloat16
F32 = jnp.float32
_NT = (((1,), (1,)), ((), ()))
_TN = (((0,), (0,)), ((), ()))


def _params(sem, vmem=VMEM_LIMIT):
    return pltpu.CompilerParams(dimension_semantics=sem, vmem_limit_bytes=vmem)


def _sigmoid(x):
    return 1.0 / (1.0 + jnp.exp(-x))


def _in_proj_kernel(xp_ref, xs_ref, w_ref, o_ref, wbf_ref):
    i = pl.program_id(1)

    @pl.when(i == 0)
    def _():
        wbf_ref[...] = w_ref[...].astype(BF16)

    @pl.when(i < N_PT)
    def _():
        o_ref[...] = jnp.dot(xp_ref[...], wbf_ref[...], preferred_element_type=F32)

    @pl.when(i == N_PT)
    def _():
        o_ref[...] = jnp.dot(xs_ref[...], wbf_ref[...], preferred_element_type=F32)


def _in_proj(xp_bf, xs_bf, w_in):
    return pl.pallas_call(
        _in_proj_kernel,
        grid=(D_IN_TOTAL // TILE, N_TOK // TILE),
        in_specs=[pl.BlockSpec((TILE, D_MODEL), lambda j, i: (jnp.minimum(i, N_PT - 1), 0)),
                  pl.BlockSpec((TILE, D_MODEL), lambda j, i: (0, 0)),
                  pl.BlockSpec((D_MODEL, TILE), lambda j, i: (0, j))],
        out_specs=pl.BlockSpec((TILE, TILE), lambda j, i: (i, j)),
        out_shape=jax.ShapeDtypeStruct((N_TOK, D_IN_TOTAL), F32),
        scratch_shapes=[pltpu.VMEM((D_MODEL, TILE), BF16)],
        compiler_params=_params(("arbitrary", "arbitrary")),
        name="in_proj",
    )(xp_bf, xs_bf, w_in)


def _conv_prompt_kernel(cb_ref, cc_ref, cx_ref, w_ref, a_ref, tail_ref, prev_ref):
    @pl.when(pl.program_id(2) == 0)
    def _():
        prev_ref[...] = jnp.zeros_like(prev_ref)

    u = cc_ref[...] * cx_ref[...]
    rows = lax.broadcasted_iota(jnp.int32, u.shape, 0)
    p0 = prev_ref[0:1, :]
    p1 = prev_ref[1:2, :]
    um1 = jnp.where(rows == 0, p1, pltpu.roll(u, 1, 0))
    um2 = jnp.where(rows == 0, p0, jnp.where(rows == 1, p1, pltpu.roll(u, 2, 0)))
    y = w_ref[0:1, :] * um2 + w_ref[1:2, :] * um1 + w_ref[2:3, :] * u
    a_ref[...] = (cb_ref[...] * y).astype(BF16)
    last = u[TILE - 2:TILE, :]
    prev_ref[0:2, :] = last
    tail_ref[...] = last


def _conv_prompt(h, conv_w):
    nr = SEQ // TILE
    nc = D_CONV // TILE

    def sec(col):
        return pl.BlockSpec((TILE, TILE), lambda b, c, r, col=col: (b * nr + r, col // TILE + c))

    return pl.pallas_call(
        _conv_prompt_kernel,
        grid=(BATCH, nc, nr),
        in_specs=[sec(COL_CB), sec(COL_CC), sec(COL_CX),
                  pl.BlockSpec((CONV_WIDTH, TILE), lambda b, c, r: (0, c))],
        out_specs=[pl.BlockSpec((TILE, TILE), lambda b, c, r: (b * nr + r, c)),
                   pl.BlockSpec((None, CONV_WIDTH - 1, TILE), lambda b, c, r: (b, 0, c))],
        out_shape=[jax.ShapeDtypeStruct((N_P, D_CONV), BF16),
                   jax.ShapeDtypeStruct((BATCH, CONV_WIDTH - 1, D_CONV), F32)],
        scratch_shapes=[pltpu.VMEM((SUBLANES, TILE), F32)],
        compiler_params=_params(("arbitrary", "arbitrary", "arbitrary")),
        name="conv_prompt",
    )(h, h, h, conv_w)


def _conv_sample_kernel(cb_ref, cc_ref, cx_ref, st_ref, w_ref, a_ref, tail_ref):
    b = DEC_BATCH
    w0, w1, w2 = w_ref[0:1, :], w_ref[1:2, :], w_ref[2:3, :]
    u = cc_ref[...] * cx_ref[...]
    up = [st_ref[0], st_ref[1]] + [u[t * b:(t + 1) * b, :] for t in range(DEC_SEQ)]
    for t in range(DEC_SEQ):
        y = w0 * up[t] + w1 * up[t + 1] + w2 * up[t + 2]
        a_ref[t * b:(t + 1) * b, :] = (cb_ref[t * b:(t + 1) * b, :] * y).astype(BF16)
    tail_ref[0] = up[DEC_SEQ]
    tail_ref[1] = up[DEC_SEQ + 1]


def _conv_sample(h, state_t, conv_w):
    nc = D_CONV // TILE

    def sec(col):
        return pl.BlockSpec((N_S, TILE), lambda c, col=col: (N_PT, col // TILE + c))

    return pl.pallas_call(
        _conv_sample_kernel,
        grid=(nc,),
        in_specs=[sec(COL_CB), sec(COL_CC), sec(COL_CX),
                  pl.BlockSpec((CONV_WIDTH - 1, DEC_BATCH, TILE), lambda c: (0, 0, c)),
                  pl.BlockSpec((CONV_WIDTH, TILE), lambda c: (0, c))],
        out_specs=[pl.BlockSpec((N_S, TILE), lambda c: (0, c)),
                   pl.BlockSpec((CONV_WIDTH - 1, DEC_BATCH, TILE), lambda c: (0, 0, c))],
        out_shape=[jax.ShapeDtypeStruct((N_S, D_CONV), BF16),
                   jax.ShapeDtypeStruct((CONV_WIDTH - 1, DEC_BATCH, D_CONV), F32)],
        compiler_params=_params(("arbitrary",)),
        name="conv_sample",
    )(h, h, h, state_t, conv_w)


def _rope_kernel(q_ref, k_ref, v_ref, c_ref, s1_ref, s2_ref, qp_ref, kp_ref, vp_ref, qs_ref, ks_ref, vs_ref):
    c, s1, s2 = c_ref[...], s1_ref[...], s2_ref[...]

    def rot(x):
        return x * c + pltpu.roll(x, LANES - ROT_DIM // 2, 1) * s1 + pltpu.roll(x, ROT_DIM // 2, 1) * s2

    def emit(qo_ref, ko_ref, vo_ref):
        for hd in range(N_HEADS):
            sl = slice(hd * HEAD_DIM, (hd + 1) * HEAD_DIM)
            qo_ref[:, sl] = rot(q_ref[:, sl])
        for hd in range(N_KV_HEADS):
            sl = slice(hd * HEAD_DIM, (hd + 1) * HEAD_DIM)
            ko_ref[:, sl] = rot(k_ref[:, sl])
        vo_ref[...] = v_ref[...]

    @pl.when(pl.program_id(0) < N_PT)
    def _():
        emit(qp_ref, kp_ref, vp_ref)

    @pl.when(pl.program_id(0) == N_PT)
    def _():
        emit(qs_ref, ks_ref, vs_ref)


def _split_specs(tm, width, n_prompt_tiles):
    return (pl.BlockSpec((tm, width), lambda i: (jnp.minimum(i, n_prompt_tiles - 1), 0)),
            pl.BlockSpec((tm, width), lambda i: (jnp.maximum(i - n_prompt_tiles, 0), 0)))


def _rope(h, tab_c, tab_s1, tab_s2):
    tiles_per_seq = SEQ // TILE

    def tab_map(i):
        return (jnp.where(i < N_PT, i % tiles_per_seq, tiles_per_seq), 0)

    tab = pl.BlockSpec((TILE, LANES), tab_map)
    qp, qs = _split_specs(TILE, D_ATTN, N_PT)
    kp, ks = _split_specs(TILE, D_KV, N_PT)
    return pl.pallas_call(
        _rope_kernel,
        grid=(N_TOK // TILE,),
        in_specs=[pl.BlockSpec((TILE, D_ATTN), lambda i: (i, COL_Q // D_ATTN)),
                  pl.BlockSpec((TILE, D_KV), lambda i: (i, COL_K // D_KV)),
                  pl.BlockSpec((TILE, D_KV), lambda i: (i, COL_V // D_KV)),
                  tab, tab, tab],
        out_specs=[qp, kp, kp, qs, ks, ks],
        out_shape=[jax.ShapeDtypeStruct((N_P, D_ATTN), F32),
                   jax.ShapeDtypeStruct((N_P, D_KV), F32),
                   jax.ShapeDtypeStruct((N_P, D_KV), F32),
                   jax.ShapeDtypeStruct((N_S, D_ATTN), F32),
                   jax.ShapeDtypeStruct((N_S, D_KV), F32),
                   jax.ShapeDtypeStruct((N_S, D_KV), F32)],
        compiler_params=_params(("arbitrary",)),
        name="rope",
    )(h, h, h, tab_c, tab_s1, tab_s2)


def _rope_tables():
    half = ROT_DIM // 2
    inv = ROPE_THETA ** (-jnp.arange(half, dtype=F32) / half)
    pos = jnp.concatenate([jnp.arange(SEQ, dtype=jnp.int32),
                           PAST_LEN + jnp.repeat(jnp.arange(DEC_SEQ, dtype=jnp.int32), DEC_BATCH)])
    ang = pos.astype(F32)[:, None] * inv[None, :]
    cos, sin = jnp.cos(ang), jnp.sin(ang)
    n = pos.shape[0]
    ones = jnp.ones((n, HEAD_DIM - ROT_DIM), F32)
    zeros = jnp.zeros((n, HEAD_DIM - ROT_DIM), F32)
    zh = jnp.zeros((n, half), F32)
    tab_c = jnp.concatenate([cos, cos, ones], 1)
    tab_s1 = jnp.concatenate([-sin, zh, zeros], 1)
    tab_s2 = jnp.concatenate([zh, sin, zeros], 1)
    return tab_c, tab_s1, tab_s2


ROWS_QP = GROUP * MOBA_BLOCK


def _attn_prompt_kernel(q_ref, k_ref, v_ref, o_ref, kmean_ref, kbf_ref, vt_ref):
    nb = SEQ // MOBA_BLOCK
    qi = pl.program_id(2)

    @pl.when(qi == 0)
    def _():
        k = k_ref[...]
        kmean_ref[...] = jnp.mean(k.reshape(nb, MOBA_BLOCK, HEAD_DIM), axis=1)
        for n in range(nb):
            rows = slice(n * MOBA_BLOCK, (n + 1) * MOBA_BLOCK)
            kbf_ref[n] = k[rows, :].astype(BF16)
            vt_ref[n] = jnp.transpose(v_ref[rows, :]).astype(BF16)

    q = jnp.concatenate([q_ref[:, g * HEAD_DIM:(g + 1) * HEAD_DIM] for g in range(GROUP)], axis=0)
    gate = lax.dot_general(kmean_ref[...], q, _NT, precision=lax.Precision.HIGHEST, preferred_element_type=F32)
    blk = lax.broadcasted_iota(jnp.int32, gate.shape, 0)
    past = blk < qi
    gate = jnp.where(past, gate, -jnp.inf)
    rank = jnp.zeros(gate.shape, jnp.int32)
    for m in range(nb):
        gm = gate[m:m + 1, :]
        beats = (gm > gate) | ((gm == gate) & (blk > m))
        rank = rank + beats.astype(jnp.int32)
    chosen_blocks = jnp.where((rank < MOBA_TOPK) & past, 1.0, 0.0)

    qs = (q * (HEAD_DIM ** -0.5)).astype(BF16)
    s = lax.dot_general(kbf_ref[qi], qs, _NT, preferred_element_type=F32)
    q_off = lax.broadcasted_iota(jnp.int32, s.shape, 1) & (MOBA_BLOCK - 1)
    s = jnp.where(lax.broadcasted_iota(jnp.int32, s.shape, 0) <= q_off, s, -jnp.inf)
    m0 = jnp.max(s, axis=0, keepdims=True)
    p = jnp.exp(s - m0)
    l0 = jnp.sum(p, axis=0, keepdims=True)
    acc0 = jnp.dot(vt_ref[qi], p.astype(BF16), preferred_element_type=F32)

    def body(n, carry):
        m, l, acc = carry
        s = lax.dot_general(kbf_ref[n], qs, _NT, preferred_element_type=F32)
        chosen = jnp.max(jnp.where(blk == n, chosen_blocks, 0.0), axis=0, keepdims=True) > 0.0
        m_new = jnp.where(chosen, jnp.maximum(m, jnp.max(s, axis=0, keepdims=True)), m)
        p = jnp.where(chosen, jnp.exp(s - m_new), 0.0)
        a = jnp.exp(m - m_new)
        l = a * l + jnp.sum(p, axis=0, keepdims=True)
        acc = a * acc + jnp.dot(vt_ref[n], p.astype(BF16), preferred_element_type=F32)
        return m_new, l, acc

    _, l, acc = lax.fori_loop(0, qi, body, (m0, l0, acc0))
    o = jnp.transpose(acc / l).astype(BF16)
    for g in range(GROUP):
        o_ref[:, g * HEAD_DIM:(g + 1) * HEAD_DIM] = o[g * MOBA_BLOCK:(g + 1) * MOBA_BLOCK, :]


def _attn_prompt(q_rot, k_rot, v):
    nq = SEQ // MOBA_BLOCK
    qw = GROUP * HEAD_DIM
    kv_spec = pl.BlockSpec((SEQ, HEAD_DIM), lambda b, kv, qi: (b, kv))
    return pl.pallas_call(
        _attn_prompt_kernel,
        grid=(BATCH, N_KV_HEADS, nq),
        in_specs=[pl.BlockSpec((MOBA_BLOCK, qw), lambda b, kv, qi: (b * nq + qi, kv)), kv_spec, kv_spec],
        out_specs=pl.BlockSpec((MOBA_BLOCK, qw), lambda b, kv, qi: (b * nq + qi, kv)),
        out_shape=jax.ShapeDtypeStruct((N_P, D_ATTN), BF16),
        scratch_shapes=[pltpu.VMEM((SEQ // MOBA_BLOCK, HEAD_DIM), F32),
                        pltpu.VMEM((SEQ // MOBA_BLOCK, MOBA_BLOCK, HEAD_DIM), BF16),
                        pltpu.VMEM((SEQ // MOBA_BLOCK, HEAD_DIM, MOBA_BLOCK), BF16)],
        compiler_params=_params(("arbitrary", "arbitrary", "arbitrary")),
        name="attn_prompt",
    )(q_rot, k_rot, v)


PAGE_ROWS = PAGE_SIZE * N_KV_HEADS
PAST_ROWS = N_PAGES * PAGE_ROWS
NEW_ROWS = DEC_SEQ * N_KV_HEADS
KEY_ROWS = PAST_ROWS + LANES
ROWS_QS = N_KV_HEADS * GROUP * DEC_SEQ
BLOCK_ROWS = MOBA_BLOCK * N_KV_HEADS


def _page_copies(cache_hbm, pt_ref, b, buf_ref, slot, sem):
    return [pltpu.make_async_copy(cache_hbm.at[pl.ds(pl.multiple_of(pt_ref[b * N_PAGES + p] * PAGE_ROWS, PAGE_ROWS),
                                                     PAGE_ROWS)],
                                  buf_ref.at[slot, pl.ds(p * PAGE_ROWS, PAGE_ROWS)], sem.at[slot])
            for p in range(N_PAGES)]


def _attn_sample_kernel(pt_ref, q_ref, kn_ref, vn_ref, ck_hbm, cv_hbm, o_ref, kbuf, vbuf, bias_ref, ksem, vsem):
    b = pl.program_id(0)
    nb = PAST_LEN // MOBA_BLOCK
    slot = b % 2

    @pl.when(b == 0)
    def _():
        for c in _page_copies(ck_hbm, pt_ref, 0, kbuf, 0, ksem) + _page_copies(cv_hbm, pt_ref, 0, vbuf, 0, vsem):
            c.start()
        zeros = jnp.zeros((KEY_ROWS - PAST_ROWS, HEAD_DIM), F32)
        for s in range(2):
            kbuf[s, PAST_ROWS:, :] = zeros
            vbuf[s, PAST_ROWS:, :] = zeros
        key = lax.broadcasted_iota(jnp.int32, (KEY_ROWS, ROWS_QS), 0)
        qrow = lax.broadcasted_iota(jnp.int32, (KEY_ROWS, ROWS_QS), 1)
        same_head = (key & (N_KV_HEADS - 1)) == lax.shift_right_logical(qrow, 4)
        new_t = lax.shift_right_arithmetic(key - PAST_ROWS, 2)
        ok = same_head & ((key < PAST_ROWS) | (new_t <= (qrow & (DEC_SEQ - 1))))
        bias_ref[...] = jnp.where(ok, 0.0, -jnp.inf)

    @pl.when(b + 1 < DEC_BATCH)
    def _():
        nxt = 1 - slot
        for c in (_page_copies(ck_hbm, pt_ref, b + 1, kbuf, nxt, ksem)
                  + _page_copies(cv_hbm, pt_ref, b + 1, vbuf, nxt, vsem)):
            c.start()

    for c in _page_copies(ck_hbm, pt_ref, b, kbuf, slot, ksem) + _page_copies(cv_hbm, pt_ref, b, vbuf, slot, vsem):
        c.wait()
    kbuf[slot, PAST_ROWS:PAST_ROWS + NEW_ROWS, :] = kn_ref[...]
    vbuf[slot, PAST_ROWS:PAST_ROWS + NEW_ROWS, :] = vn_ref[...]

    q = q_ref[...]
    k_all = kbuf[slot]
    ksum = jnp.sum(k_all[:PAST_ROWS].reshape(nb, BLOCK_ROWS // SUBLANES, SUBLANES, HEAD_DIM), axis=1)
    ksum = (ksum + pltpu.roll(ksum, N_KV_HEADS, 1)).reshape(nb * SUBLANES, HEAD_DIM)
    gate = lax.dot_general(ksum, q, _NT, precision=lax.Precision.HIGHEST, preferred_element_type=F32)
    grow = lax.broadcasted_iota(jnp.int32, gate.shape, 0)
    gq = lax.broadcasted_iota(jnp.int32, gate.shape, 1)
    mine = (grow & (SUBLANES - 1)) == lax.shift_right_logical(gq, 4)
    gblk = lax.shift_right_logical(grow, 3)
    gate = jnp.where(mine, gate, -jnp.inf)
    rank = jnp.zeros(gate.shape, jnp.int32)
    for n in range(nb):
        gn = jnp.max(jnp.where(gblk == n, gate, -jnp.inf), axis=0, keepdims=True)
        beats = (gn > gate) | ((gn == gate) & (gblk > n))
        rank = rank + beats.astype(jnp.int32)
    sel = mine & (rank < MOBA_TOPK)
    blk_bias = [jnp.where(jnp.max(jnp.where((gblk == n) & sel, 1.0, 0.0), axis=0, keepdims=True) > 0.0, 0.0, -jnp.inf)
                for n in range(nb)]

    qs = (q * (HEAD_DIM ** -0.5)).astype(BF16)
    s = lax.dot_general(k_all.astype(BF16), qs, _NT, preferred_element_type=F32) + bias_ref[...]
    parts = [s[n * BLOCK_ROWS:(n + 1) * BLOCK_ROWS] + blk_bias[n] for n in range(nb)] + [s[PAST_ROWS:]]
    s = jnp.concatenate(parts, axis=0)
    m = jnp.max(s, axis=0, keepdims=True)
    p = jnp.exp(s - m)
    l = jnp.sum(p, axis=0, keepdims=True)
    p = (p * (1.0 / l)).astype(BF16)
    o_ref[...] = lax.dot_general(p, vbuf[slot].astype(BF16), _TN, preferred_element_type=F32)


def _attn_sample(page_table, q_s, k_new, v_new, cache_k, cache_v):
    q_spec = pl.BlockSpec((None, ROWS_QS, HEAD_DIM), lambda b, pt: (b, 0, 0))
    n_spec = pl.BlockSpec((None, NEW_ROWS, HEAD_DIM), lambda b, pt: (b, 0, 0))
    any_spec = pl.BlockSpec(memory_space=pl.ANY)
    grid_spec = pltpu.PrefetchScalarGridSpec(
        num_scalar_prefetch=1,
        grid=(DEC_BATCH,),
        in_specs=[q_spec, n_spec, n_spec, any_spec, any_spec],
        out_specs=q_spec,
        scratch_shapes=[pltpu.VMEM((2, KEY_ROWS, HEAD_DIM), F32),
                        pltpu.VMEM((2, KEY_ROWS, HEAD_DIM), F32),
                        pltpu.VMEM((KEY_ROWS, ROWS_QS), F32),
                        pltpu.SemaphoreType.DMA((2,)),
                        pltpu.SemaphoreType.DMA((2,))],
    )
    return pl.pallas_call(
        _attn_sample_kernel,
        grid_spec=grid_spec,
        out_shape=jax.ShapeDtypeStruct((DEC_BATCH, ROWS_QS, HEAD_DIM), F32),
        compiler_params=_params(("arbitrary",)),
        name="attn_sample",
    )(page_table.reshape(-1), q_s, k_new, v_new, cache_k, cache_v)


def _merge_kernel(ap_ref, as_ref, tp_ref, ts_ref, wc_ref, wa_ref, gc_ref, ga_ref, o_ref, wcb_ref, wab_ref):
    i = pl.program_id(1)

    @pl.when(i == 0)
    def _():
        wcb_ref[...] = wc_ref[...].astype(BF16)
        wab_ref[...] = wa_ref[...].astype(BF16)

    def emit(a_ref, t_ref):
        ya = jnp.dot(a_ref[...], wcb_ref[...], preferred_element_type=F32)
        yb = jnp.dot(t_ref[...], wab_ref[...], preferred_element_type=F32)
        o_ref[...] = (_sigmoid(gc_ref[...]) * ya + _sigmoid(ga_ref[...]) * yb).astype(BF16)

    @pl.when(i < N_PT)
    def _():
        emit(ap_ref, tp_ref)

    @pl.when(i == N_PT)
    def _():
        emit(as_ref, ts_ref)


def _merge(a_p, a_s, attn_p, attn_s, w_conv_branch, w_attn_branch, h):
    act_p = pl.BlockSpec((TILE, D_CONV), lambda j, i: (jnp.minimum(i, N_PT - 1), 0))
    act_s = pl.BlockSpec((TILE, D_CONV), lambda j, i: (0, 0))
    wsp = pl.BlockSpec((D_CONV, TILE), lambda j, i: (0, j))

    def gate(col):
        return pl.BlockSpec((TILE, TILE), lambda j, i, col=col: (i, col // TILE + j))

    return pl.pallas_call(
        _merge_kernel,
        grid=(D_MODEL // TILE, N_TOK // TILE),
        in_specs=[act_p, act_s, act_p, act_s, wsp, wsp, gate(COL_GC), gate(COL_GA)],
        out_specs=pl.BlockSpec((TILE, TILE), lambda j, i: (i, j)),
        out_shape=jax.ShapeDtypeStruct((N_TOK, D_MODEL), BF16),
        scratch_shapes=[pltpu.VMEM((D_CONV, TILE), BF16), pltpu.VMEM((D_ATTN, TILE), BF16)],
        compiler_params=_params(("arbitrary", "arbitrary")),
        name="merge",
    )(a_p, a_s, attn_p, attn_s, w_conv_branch, w_attn_branch, h, h)


def _out_proj_kernel(m_ref, w_ref, xp_ref, xs_ref, o_ref, wbf_ref):
    i = pl.program_id(1)

    @pl.when(i == 0)
    def _():
        wbf_ref[...] = w_ref[...].astype(BF16)

    y = jnp.dot(m_ref[...], wbf_ref[...], preferred_element_type=F32)

    @pl.when(i < N_PT)
    def _():
        o_ref[...] = ALPHA * xp_ref[...] + y

    @pl.when(i == N_PT)
    def _():
        o_ref[...] = ALPHA * xs_ref[...] + y


def _out_proj(merged, w_o, x_p, x_s):
    return pl.pallas_call(
        _out_proj_kernel,
        grid=(D_MODEL // TILE, N_TOK // TILE),
        in_specs=[pl.BlockSpec((TILE, D_MODEL), lambda j, i: (i, 0)),
                  pl.BlockSpec((D_MODEL, TILE), lambda j, i: (0, j)),
                  pl.BlockSpec((TILE, TILE), lambda j, i: (jnp.minimum(i, N_PT - 1), j)),
                  pl.BlockSpec((TILE, TILE), lambda j, i: (0, j))],
        out_specs=pl.BlockSpec((TILE, TILE), lambda j, i: (i, j)),
        out_shape=jax.ShapeDtypeStruct((N_TOK, D_MODEL), F32),
        scratch_shapes=[pltpu.VMEM((D_MODEL, TILE), BF16)],
        compiler_params=_params(("arbitrary", "arbitrary")),
        name="out_proj",
    )(merged, w_o, x_p, x_s)


def _layer_norm(x, g, b):
    mu = jnp.mean(x, axis=-1, keepdims=True)
    xc = x - mu
    var = jnp.mean(xc * xc, axis=-1, keepdims=True)
    return xc * lax.rsqrt(var + LN_EPS) * g + b


def _rows_to_chunks(x, dst_ref, n_rows, first_chunk=0):
    for c in range(x.shape[1] // LANES):
        dst_ref[pl.ds(first_chunk + c, n_rows, stride=ROW_PITCH), :] = x[:, c * LANES:(c + 1) * LANES]


def _pad_chunks(dst_ref, n_rows):
    zeros = jnp.zeros((n_rows, LANES), F32)
    for c in range(ROW_CHUNKS, ROW_PITCH):
        dst_ref[pl.ds(c, n_rows, stride=ROW_PITCH), :] = zeros


def _chunks_to_rows(src_ref, n_rows):
    return [src_ref[pl.ds(c, n_rows, stride=ROW_PITCH), :] for c in range(ROW_CHUNKS)]


GATHER_UNROLL = 8


def _chunk_row_copy(src_hbm, src_row, buf_ref, dst_row, sem):
    src = pl.multiple_of(src_row * ROW_PITCH, SUBLANES)
    dst = pl.multiple_of(dst_row * ROW_PITCH, SUBLANES)
    return pltpu.make_async_copy(src_hbm.at[pl.ds(src, ROW_CHUNKS)], buf_ref.at[pl.ds(dst, ROW_CHUNKS)], sem)


def _ln_router_kernel(x_ref, g_ref, b_ref, wr_ref, br_ref, h_ref, hc_ref, e_ref, w_ref):
    h = _layer_norm(x_ref[...], g_ref[...], b_ref[...])
    h_ref[...] = h
    _rows_to_chunks(h, hc_ref, LN_TM)
    _pad_chunks(hc_ref, LN_TM)
    x = jnp.dot(h, wr_ref[...], precision=lax.Precision.HIGHEST, preferred_element_type=F32) + br_ref[...]
    lane = lax.broadcasted_iota(jnp.int32, x.shape, 1)
    lane_f = lane.astype(F32)
    ninf = -jnp.inf

    def first_lane(hit):
        return jnp.min(jnp.where(hit, lane_f, float(LANES)), axis=-1, keepdims=True)

    is_g = lane < N_GROUPS
    glog = jnp.where(is_g, x, ninf)
    gmax = jnp.max(glog, axis=-1, keepdims=True)
    grp = first_lane(glog == gmax)
    wg = 1.0 / jnp.sum(jnp.where(is_g, jnp.exp(x - gmax), 0.0), axis=-1, keepdims=True)
    lane_grp = lax.shift_right_logical(lane, 3).astype(F32)
    in_grp = (lane >= N_GROUPS) & (lane < N_GROUPS + N_EXPERTS) & (lane_grp == grp + 1.0)
    elog = jnp.where(in_grp, x, ninf)
    t1 = jnp.max(elog, axis=-1, keepdims=True)
    i1 = first_lane(elog == t1)
    elog2 = jnp.where(lane_f == i1, ninf, elog)
    t2 = jnp.max(elog2, axis=-1, keepdims=True)
    i2 = first_lane(elog2 == t2)
    e2 = jnp.exp(t2 - t1)
    den = 1.0 + e2
    e_ref[...] = jnp.where(lane == 0, i1 - N_GROUPS, jnp.where(lane == 1, i2 - N_GROUPS, 0.0)).astype(jnp.int32)
    w_ref[...] = jnp.where(lane == 0, wg * (1.0 / den), jnp.where(lane == 1, wg * (e2 / den), 0.0))


def _ln_router(pre, g, b, w_router, b_router):
    tm = LN_TM
    row = pl.BlockSpec((tm, D_MODEL), lambda i: (i, 0))
    vec = pl.BlockSpec((1, D_MODEL), lambda i: (0, 0))
    small = pl.BlockSpec((tm, LANES), lambda i: (i, 0))
    return pl.pallas_call(
        _ln_router_kernel,
        grid=(N_TOK // tm,),
        in_specs=[row, vec, vec,
                  pl.BlockSpec((D_MODEL, LANES), lambda i: (0, 0)),
                  pl.BlockSpec((1, LANES), lambda i: (0, 0))],
        out_specs=[row, pl.BlockSpec((tm * ROW_PITCH, LANES), lambda i: (i, 0)), small, small],
        out_shape=[jax.ShapeDtypeStruct((N_TOK, D_MODEL), F32),
                   jax.ShapeDtypeStruct((N_TOK * ROW_PITCH, LANES), F32),
                   jax.ShapeDtypeStruct((N_TOK, LANES), jnp.int32),
                   jax.ShapeDtypeStruct((N_TOK, LANES), F32)],
        compiler_params=_params(("arbitrary",)),
        name="ln_router",
    )(pre, g, b, w_router, b_router)


def _dispatch_kernel(nused_ref, src_ref, cnt_ref, tok_ref, hc_hbm, o_ref, buf_ref, sem):
    i = pl.program_id(0)

    @pl.when(i == 0)
    def _():
        buf_ref[...] = jnp.zeros_like(buf_ref)

    @pl.when(i < nused_ref[0])
    def _():
        cnt = cnt_ref[i]
        base = src_ref[i]
        n_groups = (cnt + GATHER_UNROLL - 1) // GATHER_UNROLL

        def issue(g, c):
            for u in range(GATHER_UNROLL):
                r = g * GATHER_UNROLL + u
                tok = tok_ref[base + jnp.minimum(r, cnt - 1)]
                _chunk_row_copy(hc_hbm, tok, buf_ref, r, sem).start(priority=u % 2)
            return c

        lax.fori_loop(0, n_groups, issue, 0)

        def wait(g, c):
            n = GATHER_UNROLL * ROW_CHUNKS
            pltpu.make_async_copy(hc_hbm.at[pl.ds(0, n)], buf_ref.at[pl.ds(0, n)], sem).wait()
            return c

        lax.fori_loop(0, n_groups, wait, 0)
        live = lax.broadcasted_iota(jnp.int32, (MOE_TM, LANES), 0) < cnt
        for c, chunk in enumerate(_chunks_to_rows(buf_ref, MOE_TM)):
            o_ref[:, c * LANES:(c + 1) * LANES] = jnp.where(live, chunk, 0.0).astype(BF16)


def _dispatch(n_used, blk_src, blk_cnt, sorted_tok, h_chunks):
    grid_spec = pltpu.PrefetchScalarGridSpec(
        num_scalar_prefetch=3,
        grid=(MOE_NB,),
        in_specs=[pl.BlockSpec(memory_space=pltpu.SMEM),
                  pl.BlockSpec(memory_space=pl.ANY)],
        out_specs=pl.BlockSpec((MOE_TM, D_MODEL), lambda i, nu, src, cnt: (jnp.minimum(i, nu[0] - 1), 0)),
        scratch_shapes=[pltpu.VMEM((MOE_TM * ROW_PITCH, LANES), F32), pltpu.SemaphoreType.DMA(())],
    )
    return pl.pallas_call(
        _dispatch_kernel,
        grid_spec=grid_spec,
        out_shape=jax.ShapeDtypeStruct((MOE_P, D_MODEL), BF16),
        compiler_params=_params(("arbitrary",)),
        name="moe_dispatch",
    )(n_used, blk_src, blk_cnt, sorted_tok, h_chunks)


def _mlp_up_kernel(blk_ref, f_ref, e_ref, new_ref, x_ref, wg_ref, wu_ref, o_ref, wgb_ref, wub_ref):
    del blk_ref, f_ref, e_ref

    @pl.when(new_ref[pl.program_id(0)] == 1)
    def _():
        wgb_ref[...] = wg_ref[...].astype(BF16)
        wub_ref[...] = wu_ref[...].astype(BF16)

    x = x_ref[...]
    g = jnp.dot(x, wgb_ref[...], preferred_element_type=F32)
    u = jnp.dot(x, wub_ref[...], preferred_element_type=F32)
    o_ref[...] = (g * _sigmoid(g) * u).astype(BF16)


def _mlp_up(n_items, items, xs, w_gate, w_up):
    wsp = pl.BlockSpec((None, D_MODEL, MOE_TF), lambda t, blk, f, e, n: (e[t], 0, f[t]))
    grid_spec = pltpu.PrefetchScalarGridSpec(
        num_scalar_prefetch=4,
        grid=(n_items,),
        in_specs=[pl.BlockSpec((MOE_TM, D_MODEL), lambda t, blk, f, e, n: (blk[t], 0)), wsp, wsp],
        out_specs=pl.BlockSpec((MOE_TM, MOE_TF), lambda t, blk, f, e, n: (blk[t], f[t])),
        scratch_shapes=[pltpu.VMEM((D_MODEL, MOE_TF), BF16), pltpu.VMEM((D_MODEL, MOE_TF), BF16)],
    )
    return pl.pallas_call(
        _mlp_up_kernel,
        grid_spec=grid_spec,
        out_shape=jax.ShapeDtypeStruct((MOE_P, D_EXPERT), BF16),
        compiler_params=_params(("arbitrary",)),
        name="moe_up",
    )(*items, xs, w_gate, w_up)


def _mlp_down_kernel(blk_ref, f_ref, e_ref, new_ref, x_ref, wd_ref, o_ref, wdb_ref):
    del blk_ref, f_ref, e_ref

    @pl.when(new_ref[pl.program_id(0)] == 1)
    def _():
        wdb_ref[...] = wd_ref[...].astype(BF16)

    o_ref[...] = jnp.dot(x_ref[...], wdb_ref[...], preferred_element_type=F32)


def _mlp_down(n_items, items, hmid, w_down):
    grid_spec = pltpu.PrefetchScalarGridSpec(
        num_scalar_prefetch=4,
        grid=(n_items,),
        in_specs=[pl.BlockSpec((MOE_TM, D_EXPERT), lambda t, blk, f, e, n: (blk[t], 0)),
                  pl.BlockSpec((None, D_EXPERT, MOE_TN), lambda t, blk, f, e, n: (e[t], 0, f[t]))],
        out_specs=pl.BlockSpec((MOE_TM, MOE_TN), lambda t, blk, f, e, n: (blk[t], f[t])),
        scratch_shapes=[pltpu.VMEM((D_EXPERT, MOE_TN), BF16)],
    )
    return pl.pallas_call(
        _mlp_down_kernel,
        grid_spec=grid_spec,
        out_shape=jax.ShapeDtypeStruct((MOE_P, D_MODEL), F32),
        compiler_params=_params(("arbitrary",)),
        name="moe_down",
    )(*items, hmid, w_down)


def _combine_kernel(dest_ref, w_ref, h_ref, g_ref, b_ref, ys_hbm, zp_ref, zs_ref, buf_ref, sem):
    i = pl.program_id(0)

    def row_copy(src_row, k, r):
        return pltpu.make_async_copy(ys_hbm.at[pl.ds(src_row, 1)], buf_ref.at[k, pl.ds(r, 1)], sem)

    def issue(g, c):
        for u in range(GATHER_UNROLL):
            r = g * GATHER_UNROLL + u
            for k in range(TOPK_IN_GROUP):
                row_copy(dest_ref[0, TOPK_IN_GROUP * r + k], k, r).start(priority=(u + k) % 2)
        return c

    lax.fori_loop(0, LN_TM // GATHER_UNROLL, issue, 0)
    for k in range(TOPK_IN_GROUP):
        pltpu.make_async_copy(ys_hbm.at[pl.ds(0, LN_TM)], buf_ref.at[k], sem).wait()
    w = w_ref[...]
    moe = buf_ref[0] * w[:, 0:1] + buf_ref[1] * w[:, 1:2]
    z = _layer_norm(ALPHA * h_ref[...] + moe, g_ref[...], b_ref[...])

    @pl.when(i < LN_PT)
    def _():
        zp_ref[...] = z

    @pl.when(i >= LN_PT)
    def _():
        zs_ref[...] = z


def _combine(dest, wts, h1, g, b, ys):
    tm = LN_TM
    row = pl.BlockSpec((tm, D_MODEL), lambda i: (i, 0))
    vec = pl.BlockSpec((1, D_MODEL), lambda i: (0, 0))
    zp, zs = _split_specs(tm, D_MODEL, LN_PT)
    return pl.pallas_call(
        _combine_kernel,
        grid=(N_TOK // tm,),
        in_specs=[pl.BlockSpec((None, 1, TOPK_IN_GROUP * tm), lambda i: (i, 0, 0), memory_space=pltpu.SMEM),
                  pl.BlockSpec((tm, LANES), lambda i: (i, 0)),
                  row, vec, vec,
                  pl.BlockSpec(memory_space=pl.ANY)],
        out_specs=[zp, zs],
        out_shape=[jax.ShapeDtypeStruct((N_P, D_MODEL), F32), jax.ShapeDtypeStruct((N_S, D_MODEL), F32)],
        scratch_shapes=[pltpu.VMEM((TOPK_IN_GROUP, tm, D_MODEL), F32), pltpu.SemaphoreType.DMA(())],
        compiler_params=_params(("arbitrary",)),
        name="moe_combine",
    )(dest.reshape(N_TOK // tm, 1, TOPK_IN_GROUP * tm), wts, h1, g, b, ys)


def _moe_tables(eid):
    flat_e = eid.reshape(-1)
    experts = jnp.arange(N_EXPERTS, dtype=jnp.int32)
    iota_a = jnp.arange(N_ASSIGN, dtype=jnp.int32)
    se, order = lax.sort((flat_e, iota_a), num_keys=1)
    counts = jnp.sum((flat_e[:, None] == experts[None, :]).astype(jnp.int32), axis=0)
    nblk = (counts + MOE_TM - 1) // MOE_TM
    blk_end = jnp.cumsum(nblk)
    blk_start = blk_end - nblk
    start = jnp.cumsum(counts) - counts
    n_used = blk_end[-1]
    shift = blk_start * MOE_TM - start
    dest_sorted = iota_a + jnp.sum(jnp.where(se[:, None] == experts[None, :], shift[None, :], 0), axis=1)
    _, dest = lax.sort((order, dest_sorted), num_keys=1)
    blks = jnp.arange(MOE_NB, dtype=jnp.int32)
    b_e = jnp.minimum(jnp.sum((blk_end[None, :] <= blks[:, None]).astype(jnp.int32), axis=1), N_EXPERTS - 1)
    b_first = (blks - blk_start[b_e]) * MOE_TM
    blk_src = jnp.clip(start[b_e] + b_first, 0, N_ASSIGN - 1)
    blk_cnt = jnp.clip(counts[b_e] - b_first, 1, MOE_TM)
    sorted_tok = order // TOPK_IN_GROUP

    def items(n_inner):
        n_run = n_used * n_inner
        t = jnp.arange(MOE_NB * n_inner, dtype=jnp.int32)
        tc = jnp.minimum(t, n_run - 1)
        e = jnp.minimum(jnp.sum((blk_end[None, :] * n_inner <= tc[:, None]).astype(jnp.int32), axis=1), N_EXPERTS - 1)
        q = tc - blk_start[e] * n_inner
        nb_e = jnp.maximum(nblk[e], 1)
        f, j = q // nb_e, q % nb_e
        blk = blk_start[e] + j
        prev_e = jnp.concatenate([jnp.full((1,), -1, jnp.int32), e[:-1]])
        prev_f = jnp.concatenate([jnp.full((1,), -1, jnp.int32), f[:-1]])
        new = (e != prev_e) | (f != prev_f)
        return n_run.astype(jnp.int32), (blk.astype(jnp.int32), f.astype(jnp.int32), e.astype(jnp.int32),
                                         new.astype(jnp.int32))

    dispatch_tables = (n_used.reshape(1).astype(jnp.int32), blk_src.astype(jnp.int32), blk_cnt.astype(jnp.int32),
                       sorted_tok.astype(jnp.int32))
    return dispatch_tables, dest.astype(jnp.int32), items


def kernel(x_prompt, x_sample, cache_k, cache_v, state_conv, page_table, w_in, conv_w, w_conv_branch, w_attn_branch, w_o, ln1_g, ln1_b, w_router_group, b_router_group, w_router_expert, b_router_expert, w_gate, w_up, w_down, ln2_g, ln2_b):
    l = 0
    x_p = x_prompt.reshape(N_P, D_MODEL)
    x_s = x_sample.transpose(1, 0, 2).reshape(N_S, D_MODEL)
    h = _in_proj(x_p.astype(BF16), x_s.astype(BF16), w_in[l])

    a_p, conv_p = _conv_prompt(h, conv_w[l])
    a_s, conv_s_t = _conv_sample(h, state_conv[l].transpose(1, 0, 2), conv_w[l])

    q_p, k_p, v_p, q_s, k_s, v_s = _rope(h, *_rope_tables())
    attn_p = _attn_prompt(q_p, k_p, v_p)

    def seq_major(a, width):
        return a.reshape(DEC_SEQ, DEC_BATCH, width // HEAD_DIM, HEAD_DIM).transpose(1, 0, 2, 3)

    k_s_b, v_s_b = seq_major(k_s, D_KV), seq_major(v_s, D_KV)
    q_s_b = seq_major(q_s, D_ATTN).transpose(0, 2, 1, 3).reshape(DEC_BATCH, ROWS_QS, HEAD_DIM)
    pool_rows = cache_k.shape[1] * PAGE_ROWS
    o_s = _attn_sample(page_table, q_s_b,
                       k_s_b.reshape(DEC_BATCH, NEW_ROWS, HEAD_DIM), v_s_b.reshape(DEC_BATCH, NEW_ROWS, HEAD_DIM),
                       cache_k[l].reshape(pool_rows, HEAD_DIM), cache_v[l].reshape(pool_rows, HEAD_DIM))
    attn_s = (o_s.reshape(DEC_BATCH, N_HEADS, DEC_SEQ, HEAD_DIM).transpose(2, 0, 1, 3)
              .reshape(N_S, D_ATTN).astype(BF16))

    merged = _merge(a_p, a_s, attn_p, attn_s, w_conv_branch[l], w_attn_branch[l], h)
    pre = _out_proj(merged, w_o[l], x_p, x_s)

    w_router = jnp.pad(jnp.concatenate([w_router_group[l], w_router_expert[l]], axis=1),
                       ((0, 0), (0, LANES - N_GROUPS - N_EXPERTS)))
    b_router = jnp.pad(jnp.concatenate([b_router_group[l], b_router_expert[l]]),
                       (0, LANES - N_GROUPS - N_EXPERTS)).reshape(1, LANES)
    h1, h_chunks, eid, wts = _ln_router(pre, ln1_g[l].reshape(1, D_MODEL), ln1_b[l].reshape(1, D_MODEL),
                                        w_router, b_router)
    dispatch_tables, dest, items = _moe_tables(eid[:, :TOPK_IN_GROUP])
    xs = _dispatch(*dispatch_tables, h_chunks)
    hmid = _mlp_up(*items(D_EXPERT // MOE_TF), xs, w_gate[l], w_up[l])
    ys = _mlp_down(*items(D_MODEL // MOE_TN), hmid, w_down[l])
    z_p, z_s = _combine(dest, wts, h1, ln2_g[l].reshape(1, D_MODEL), ln2_b[l].reshape(1, D_MODEL), ys)

    y_prompt = z_p.reshape(BATCH, SEQ, D_MODEL)
    y_sample = z_s.reshape(DEC_SEQ, DEC_BATCH, D_MODEL).transpose(1, 0, 2)
    k_prompt = k_p.reshape(1, BATCH, SEQ, N_KV_HEADS, HEAD_DIM)
    v_prompt = v_p.reshape(1, BATCH, SEQ, N_KV_HEADS, HEAD_DIM)
    conv_prompt = conv_p[None]
    conv_sample = conv_s_t.transpose(1, 0, 2)[None]
    return (y_prompt, y_sample, k_prompt, v_prompt, conv_prompt, k_s_b[None], v_s_b[None], conv_sample)
```

```python
import jax
import jax.numpy as jnp
from jax import lax
from jax.experimental import pallas as pl
from jax.experimental.pallas import tpu as pltpu

D_MODEL = 4096
BATCH = 4
SEQ = 2048
DEC_BATCH = 128
DEC_SEQ = 4
PAST_LEN = 2048
PAGE_SIZE = 128
N_PAGES = PAST_LEN // PAGE_SIZE
D_CONV = D_MODEL // 2
CONV_WIDTH = 3
HEAD_DIM = 128
N_HEADS = 16
N_KV_HEADS = 4
GROUP = N_HEADS // N_KV_HEADS
D_ATTN = N_HEADS * HEAD_DIM
D_KV = N_KV_HEADS * HEAD_DIM
ROT_DIM = HEAD_DIM // 4
ROPE_THETA = 500000.0
MOBA_BLOCK = 256
MOBA_TOPK = 3
N_GROUPS = 8
EXPERTS_PER_GROUP = 8
N_EXPERTS = N_GROUPS * EXPERTS_PER_GROUP
TOPK_IN_GROUP = 2
D_EXPERT = D_MODEL // 4
DEPTH = 1
ALPHA = (2 * DEPTH) ** 0.25
LN_EPS = 1e-5
D_IN_TOTAL = 3 * D_CONV + D_ATTN + 2 * D_KV + 2 * D_MODEL

N_P = BATCH * SEQ
N_S = DEC_BATCH * DEC_SEQ
N_TOK = N_P + N_S
N_ASSIGN = N_TOK * TOPK_IN_GROUP

COL_CB, COL_CC, COL_CX = 0, D_CONV, 2 * D_CONV
COL_Q = 3 * D_CONV
COL_K = COL_Q + D_ATTN
COL_V = COL_K + D_KV
COL_GC = COL_V + D_KV
COL_GA = COL_GC + D_MODEL

LANES = 128
SUBLANES = 8
TILE = 512
N_PT = N_P // TILE
LN_TM = 256
LN_PT = N_P // LN_TM
MOE_TM = 384
MOE_NB = -(-N_ASSIGN // MOE_TM) + N_EXPERTS
MOE_P = MOE_NB * MOE_TM
MOE_TF = 256
MOE_TN = 2048
ROW_CHUNKS = D_MODEL // LANES
ROW_PITCH = 40
VMEM_LIMIT = 56 * 1024 * 1024

BF16 = jnp.bfloat16
F32 = jnp.float32
_NT = (((1,), (1,)), ((), ()))
_TN = (((0,), (0,)), ((), ()))


def _params(sem, vmem=VMEM_LIMIT):
    return pltpu.CompilerParams(dimension_semantics=sem, vmem_limit_bytes=vmem)


def _sigmoid(x):
    return 1.0 / (1.0 + jnp.exp(-x))


def _in_proj_kernel(xp_ref, xs_ref, w_ref, o_ref, wbf_ref):
    i = pl.program_id(1)

    @pl.when(i == 0)
    def _():
        wbf_ref[...] = w_ref[...].astype(BF16)

    @pl.when(i < N_PT)
    def _():
        o_ref[...] = jnp.dot(xp_ref[...], wbf_ref[...], preferred_element_type=F32)

    @pl.when(i == N_PT)
    def _():
        o_ref[...] = jnp.dot(xs_ref[...], wbf_ref[...], preferred_element_type=F32)


def _in_proj(xp_bf, xs_bf, w_in):
    return pl.pallas_call(
        _in_proj_kernel,
        grid=(D_IN_TOTAL // TILE, N_TOK // TILE),
        in_specs=[pl.BlockSpec((TILE, D_MODEL), lambda j, i: (jnp.minimum(i, N_PT - 1), 0)),
                  pl.BlockSpec((TILE, D_MODEL), lambda j, i: (0, 0)),
                  pl.BlockSpec((D_MODEL, TILE), lambda j, i: (0, j))],
        out_specs=pl.BlockSpec((TILE, TILE), lambda j, i: (i, j)),
        out_shape=jax.ShapeDtypeStruct((N_TOK, D_IN_TOTAL), F32),
        scratch_shapes=[pltpu.VMEM((D_MODEL, TILE), BF16)],
        compiler_params=_params(("arbitrary", "arbitrary")),
        name="in_proj",
    )(xp_bf, xs_bf, w_in)


def _conv_prompt_kernel(cb_ref, cc_ref, cx_ref, w_ref, a_ref, tail_ref, prev_ref):
    @pl.when(pl.program_id(2) == 0)
    def _():
        prev_ref[...] = jnp.zeros_like(prev_ref)

    u = cc_ref[...] * cx_ref[...]
    rows = lax.broadcasted_iota(jnp.int32, u.shape, 0)
    p0 = prev_ref[0:1, :]
    p1 = prev_ref[1:2, :]
    um1 = jnp.where(rows == 0, p1, pltpu.roll(u, 1, 0))
    um2 = jnp.where(rows == 0, p0, jnp.where(rows == 1, p1, pltpu.roll(u, 2, 0)))
    y = w_ref[0:1, :] * um2 + w_ref[1:2, :] * um1 + w_ref[2:3, :] * u
    a_ref[...] = (cb_ref[...] * y).astype(BF16)
    last = u[TILE - 2:TILE, :]
    prev_ref[0:2, :] = last
    tail_ref[...] = last


def _conv_prompt(h, conv_w):
    nr = SEQ // TILE
    nc = D_CONV // TILE

    def sec(col):
        return pl.BlockSpec((TILE, TILE), lambda b, c, r, col=col: (b * nr + r, col // TILE + c))

    return pl.pallas_call(
        _conv_prompt_kernel,
        grid=(BATCH, nc, nr),
        in_specs=[sec(COL_CB), sec(COL_CC), sec(COL_CX),
                  pl.BlockSpec((CONV_WIDTH, TILE), lambda b, c, r: (0, c))],
        out_specs=[pl.BlockSpec((TILE, TILE), lambda b, c, r: (b * nr + r, c)),
                   pl.BlockSpec((None, CONV_WIDTH - 1, TILE), lambda b, c, r: (b, 0, c))],
        out_shape=[jax.ShapeDtypeStruct((N_P, D_CONV), BF16),
                   jax.ShapeDtypeStruct((BATCH, CONV_WIDTH - 1, D_CONV), F32)],
        scratch_shapes=[pltpu.VMEM((SUBLANES, TILE), F32)],
        compiler_params=_params(("arbitrary", "arbitrary", "arbitrary")),
        name="conv_prompt",
    )(h, h, h, conv_w)


def _conv_sample_kernel(cb_ref, cc_ref, cx_ref, st_ref, w_ref, a_ref, tail_ref):
    b = DEC_BATCH
    w0, w1, w2 = w_ref[0:1, :], w_ref[1:2, :], w_ref[2:3, :]
    u = cc_ref[...] * cx_ref[...]
    up = [st_ref[0], st_ref[1]] + [u[t * b:(t + 1) * b, :] for t in range(DEC_SEQ)]
    for t in range(DEC_SEQ):
        y = w0 * up[t] + w1 * up[t + 1] + w2 * up[t + 2]
        a_ref[t * b:(t + 1) * b, :] = (cb_ref[t * b:(t + 1) * b, :] * y).astype(BF16)
    tail_ref[0] = up[DEC_SEQ]
    tail_ref[1] = up[DEC_SEQ + 1]


def _conv_sample(h, state_t, conv_w):
    nc = D_CONV // TILE

    def sec(col):
        return pl.BlockSpec((N_S, TILE), lambda c, col=col: (N_PT, col // TILE + c))

    return pl.pallas_call(
        _conv_sample_kernel,
        grid=(nc,),
        in_specs=[sec(COL_CB), sec(COL_CC), sec(COL_CX),
                  pl.BlockSpec((CONV_WIDTH - 1, DEC_BATCH, TILE), lambda c: (0, 0, c)),
                  pl.BlockSpec((CONV_WIDTH, TILE), lambda c: (0, c))],
        out_specs=[pl.BlockSpec((N_S, TILE), lambda c: (0, c)),
                   pl.BlockSpec((CONV_WIDTH - 1, DEC_BATCH, TILE), lambda c: (0, 0, c))],
        out_shape=[jax.ShapeDtypeStruct((N_S, D_CONV), BF16),
                   jax.ShapeDtypeStruct((CONV_WIDTH - 1, DEC_BATCH, D_CONV), F32)],
        compiler_params=_params(("arbitrary",)),
        name="conv_sample",
    )(h, h, h, state_t, conv_w)


def _rope_kernel(q_ref, k_ref, v_ref, c_ref, s1_ref, s2_ref, qp_ref, kp_ref, vp_ref, qs_ref, ks_ref, vs_ref):
    c, s1, s2 = c_ref[...], s1_ref[...], s2_ref[...]

    def rot(x):
        return x * c + pltpu.roll(x, LANES - ROT_DIM // 2, 1) * s1 + pltpu.roll(x, ROT_DIM // 2, 1) * s2

    def emit(qo_ref, ko_ref, vo_ref):
        for hd in range(N_HEADS):
            sl = slice(hd * HEAD_DIM, (hd + 1) * HEAD_DIM)
            qo_ref[:, sl] = rot(q_ref[:, sl])
        for hd in range(N_KV_HEADS):
            sl = slice(hd * HEAD_DIM, (hd + 1) * HEAD_DIM)
            ko_ref[:, sl] = rot(k_ref[:, sl])
        vo_ref[...] = v_ref[...]

    @pl.when(pl.program_id(0) < N_PT)
    def _():
        emit(qp_ref, kp_ref, vp_ref)

    @pl.when(pl.program_id(0) == N_PT)
    def _():
        emit(qs_ref, ks_ref, vs_ref)


def _split_specs(tm, width, n_prompt_tiles):
    return (pl.BlockSpec((tm, width), lambda i: (jnp.minimum(i, n_prompt_tiles - 1), 0)),
            pl.BlockSpec((tm, width), lambda i: (jnp.maximum(i - n_prompt_tiles, 0), 0)))


def _rope(h, tab_c, tab_s1, tab_s2):
    tiles_per_seq = SEQ // TILE

    def tab_map(i):
        return (jnp.where(i < N_PT, i % tiles_per_seq, tiles_per_seq), 0)

    tab = pl.BlockSpec((TILE, LANES), tab_map)
    qp, qs = _split_specs(TILE, D_ATTN, N_PT)
    kp, ks = _split_specs(TILE, D_KV, N_PT)
    return pl.pallas_call(
        _rope_kernel,
        grid=(N_TOK // TILE,),
        in_specs=[pl.BlockSpec((TILE, D_ATTN), lambda i: (i, COL_Q // D_ATTN)),
                  pl.BlockSpec((TILE, D_KV), lambda i: (i, COL_K // D_KV)),
                  pl.BlockSpec((TILE, D_KV), lambda i: (i, COL_V // D_KV)),
                  tab, tab, tab],
        out_specs=[qp, kp, kp, qs, ks, ks],
        out_shape=[jax.ShapeDtypeStruct((N_P, D_ATTN), F32),
                   jax.ShapeDtypeStruct((N_P, D_KV), F32),
                   jax.ShapeDtypeStruct((N_P, D_KV), F32),
                   jax.ShapeDtypeStruct((N_S, D_ATTN), F32),
                   jax.ShapeDtypeStruct((N_S, D_KV), F32),
                   jax.ShapeDtypeStruct((N_S, D_KV), F32)],
        compiler_params=_params(("arbitrary",)),
        name="rope",
    )(h, h, h, tab_c, tab_s1, tab_s2)


def _rope_tables():
    half = ROT_DIM // 2
    inv = ROPE_THETA ** (-jnp.arange(half, dtype=F32) / half)
    pos = jnp.concatenate([jnp.arange(SEQ, dtype=jnp.int32),
                           PAST_LEN + jnp.repeat(jnp.arange(DEC_SEQ, dtype=jnp.int32), DEC_BATCH)])
    ang = pos.astype(F32)[:, None] * inv[None, :]
    cos, sin = jnp.cos(ang), jnp.sin(ang)
    n = pos.shape[0]
    ones = jnp.ones((n, HEAD_DIM - ROT_DIM), F32)
    zeros = jnp.zeros((n, HEAD_DIM - ROT_DIM), F32)
    zh = jnp.zeros((n, half), F32)
    tab_c = jnp.concatenate([cos, cos, ones], 1)
    tab_s1 = jnp.concatenate([-sin, zh, zeros], 1)
    tab_s2 = jnp.concatenate([zh, sin, zeros], 1)
    return tab_c, tab_s1, tab_s2


ROWS_QP = GROUP * MOBA_BLOCK


def _attn_prompt_kernel(q_ref, k_ref, v_ref, o_ref, kmean_ref, kbf_ref, vt_ref):
    nb = SEQ // MOBA_BLOCK
    qi = pl.program_id(2)

    @pl.when(qi == 0)
    def _():
        k = k_ref[...]
        kmean_ref[...] = jnp.mean(k.reshape(nb, MOBA_BLOCK, HEAD_DIM), axis=1)
        for n in range(nb):
            rows = slice(n * MOBA_BLOCK, (n + 1) * MOBA_BLOCK)
            kbf_ref[n] = k[rows, :].astype(BF16)
            vt_ref[n] = jnp.transpose(v_ref[rows, :]).astype(BF16)

    q = jnp.concatenate([q_ref[:, g * HEAD_DIM:(g + 1) * HEAD_DIM] for g in range(GROUP)], axis=0)
    gate = lax.dot_general(kmean_ref[...], q, _NT, precision=lax.Precision.HIGHEST, preferred_element_type=F32)
    blk = lax.broadcasted_iota(jnp.int32, gate.shape, 0)
    past = blk < qi
    gate = jnp.where(past, gate, -jnp.inf)
    rank = jnp.zeros(gate.shape, jnp.int32)
    for m in range(nb):
        gm = gate[m:m + 1, :]
        beats = (gm > gate) | ((gm == gate) & (blk > m))
        rank = rank + beats.astype(jnp.int32)
    chosen_blocks = jnp.where((rank < MOBA_TOPK) & past, 1.0, 0.0)

    qs = (q * (HEAD_DIM ** -0.5)).astype(BF16)
    s = lax.dot_general(kbf_ref[qi], qs, _NT, preferred_element_type=F32)
    q_off = lax.broadcasted_iota(jnp.int32, s.shape, 1) & (MOBA_BLOCK - 1)
    s = jnp.where(lax.broadcasted_iota(jnp.int32, s.shape, 0) <= q_off, s, -jnp.inf)
    m0 = jnp.max(s, axis=0, keepdims=True)
    p = jnp.exp(s - m0)
    l0 = jnp.sum(p, axis=0, keepdims=True)
    acc0 = jnp.dot(vt_ref[qi], p.astype(BF16), preferred_element_type=F32)

    def body(n, carry):
        m, l, acc = carry
        s = lax.dot_general(kbf_ref[n], qs, _NT, preferred_element_type=F32)
        chosen = jnp.max(jnp.where(blk == n, chosen_blocks, 0.0), axis=0, keepdims=True) > 0.0
        m_new = jnp.where(chosen, jnp.maximum(m, jnp.max(s, axis=0, keepdims=True)), m)
        p = jnp.where(chosen, jnp.exp(s - m_new), 0.0)
        a = jnp.exp(m - m_new)
        l = a * l + jnp.sum(p, axis=0, keepdims=True)
        acc = a * acc + jnp.dot(vt_ref[n], p.astype(BF16), preferred_element_type=F32)
        return m_new, l, acc

    _, l, acc = lax.fori_loop(0, qi, body, (m0, l0, acc0))
    o = jnp.transpose(acc / l).astype(BF16)
    for g in range(GROUP):
        o_ref[:, g * HEAD_DIM:(g + 1) * HEAD_DIM] = o[g * MOBA_BLOCK:(g + 1) * MOBA_BLOCK, :]


def _attn_prompt(q_rot, k_rot, v):
    nq = SEQ // MOBA_BLOCK
    qw = GROUP * HEAD_DIM
    kv_spec = pl.BlockSpec((SEQ, HEAD_DIM), lambda b, kv, qi: (b, kv))
    return pl.pallas_call(
        _attn_prompt_kernel,
        grid=(BATCH, N_KV_HEADS, nq),
        in_specs=[pl.BlockSpec((MOBA_BLOCK, qw), lambda b, kv, qi: (b * nq + qi, kv)), kv_spec, kv_spec],
        out_specs=pl.BlockSpec((MOBA_BLOCK, qw), lambda b, kv, qi: (b * nq + qi, kv)),
        out_shape=jax.ShapeDtypeStruct((N_P, D_ATTN), BF16),
        scratch_shapes=[pltpu.VMEM((SEQ // MOBA_BLOCK, HEAD_DIM), F32),
                        pltpu.VMEM((SEQ // MOBA_BLOCK, MOBA_BLOCK, HEAD_DIM), BF16),
                        pltpu.VMEM((SEQ // MOBA_BLOCK, HEAD_DIM, MOBA_BLOCK), BF16)],
        compiler_params=_params(("arbitrary", "arbitrary", "arbitrary")),
        name="attn_prompt",
    )(q_rot, k_rot, v)


PAGE_ROWS = PAGE_SIZE * N_KV_HEADS
PAST_ROWS = N_PAGES * PAGE_ROWS
NEW_ROWS = DEC_SEQ * N_KV_HEADS
KEY_ROWS = PAST_ROWS + LANES
ROWS_QS = N_KV_HEADS * GROUP * DEC_SEQ
BLOCK_ROWS = MOBA_BLOCK * N_KV_HEADS


def _page_copies(cache_hbm, pt_ref, b, buf_ref, slot, sem):
    return [pltpu.make_async_copy(cache_hbm.at[pl.ds(pl.multiple_of(pt_ref[b * N_PAGES + p] * PAGE_ROWS, PAGE_ROWS),
                                                     PAGE_ROWS)],
                                  buf_ref.at[slot, pl.ds(p * PAGE_ROWS, PAGE_ROWS)], sem.at[slot])
            for p in range(N_PAGES)]


def _attn_sample_kernel(pt_ref, q_ref, kn_ref, vn_ref, ck_hbm, cv_hbm, o_ref, kbuf, vbuf, bias_ref, ksem, vsem):
    b = pl.program_id(0)
    nb = PAST_LEN // MOBA_BLOCK
    slot = b % 2

    @pl.when(b == 0)
    def _():
        for c in _page_copies(ck_hbm, pt_ref, 0, kbuf, 0, ksem) + _page_copies(cv_hbm, pt_ref, 0, vbuf, 0, vsem):
            c.start()
        zeros = jnp.zeros((KEY_ROWS - PAST_ROWS, HEAD_DIM), F32)
        for s in range(2):
            kbuf[s, PAST_ROWS:, :] = zeros
            vbuf[s, PAST_ROWS:, :] = zeros
        key = lax.broadcasted_iota(jnp.int32, (KEY_ROWS, ROWS_QS), 0)
        qrow = lax.broadcasted_iota(jnp.int32, (KEY_ROWS, ROWS_QS), 1)
        same_head = (key & (N_KV_HEADS - 1)) == lax.shift_right_logical(qrow, 4)
        new_t = lax.shift_right_arithmetic(key - PAST_ROWS, 2)
        ok = same_head & ((key < PAST_ROWS) | (new_t <= (qrow & (DEC_SEQ - 1))))
        bias_ref[...] = jnp.where(ok, 0.0, -jnp.inf)

    @pl.when(b + 1 < DEC_BATCH)
    def _():
        nxt = 1 - slot
        for c in (_page_copies(ck_hbm, pt_ref, b + 1, kbuf, nxt, ksem)
                  + _page_copies(cv_hbm, pt_ref, b + 1, vbuf, nxt, vsem)):
            c.start()

    for c in _page_copies(ck_hbm, pt_ref, b, kbuf, slot, ksem) + _page_copies(cv_hbm, pt_ref, b, vbuf, slot, vsem):
        c.wait()
    kbuf[slot, PAST_ROWS:PAST_ROWS + NEW_ROWS, :] = kn_ref[...]
    vbuf[slot, PAST_ROWS:PAST_ROWS + NEW_ROWS, :] = vn_ref[...]

    q = q_ref[...]
    k_all = kbuf[slot]
    ksum = jnp.sum(k_all[:PAST_ROWS].reshape(nb, BLOCK_ROWS // SUBLANES, SUBLANES, HEAD_DIM), axis=1)
    ksum = (ksum + pltpu.roll(ksum, N_KV_HEADS, 1)).reshape(nb * SUBLANES, HEAD_DIM)
    gate = lax.dot_general(ksum, q, _NT, precision=lax.Precision.HIGHEST, preferred_element_type=F32)
    grow = lax.broadcasted_iota(jnp.int32, gate.shape, 0)
    gq = lax.broadcasted_iota(jnp.int32, gate.shape, 1)
    mine = (grow & (SUBLANES - 1)) == lax.shift_right_logical(gq, 4)
    gblk = lax.shift_right_logical(grow, 3)
    gate = jnp.where(mine, gate, -jnp.inf)
    rank = jnp.zeros(gate.shape, jnp.int32)
    for n in range(nb):
        gn = jnp.max(jnp.where(gblk == n, gate, -jnp.inf), axis=0, keepdims=True)
        beats = (gn > gate) | ((gn == gate) & (gblk > n))
        rank = rank + beats.astype(jnp.int32)
    sel = mine & (rank < MOBA_TOPK)
    blk_bias = [jnp.where(jnp.max(jnp.where((gblk == n) & sel, 1.0, 0.0), axis=0, keepdims=True) > 0.0, 0.0, -jnp.inf)
                for n in range(nb)]

    qs = (q * (HEAD_DIM ** -0.5)).astype(BF16)
    s = lax.dot_general(k_all.astype(BF16), qs, _NT, preferred_element_type=F32) + bias_ref[...]
    parts = [s[n * BLOCK_ROWS:(n + 1) * BLOCK_ROWS] + blk_bias[n] for n in range(nb)] + [s[PAST_ROWS:]]
    s = jnp.concatenate(parts, axis=0)
    m = jnp.max(s, axis=0, keepdims=True)
    p = jnp.exp(s - m)
    l = jnp.sum(p, axis=0, keepdims=True)
    p = (p * (1.0 / l)).astype(BF16)
    o_ref[...] = lax.dot_general(p, vbuf[slot].astype(BF16), _TN, preferred_element_type=F32)


def _attn_sample(page_table, q_s, k_new, v_new, cache_k, cache_v):
    q_spec = pl.BlockSpec((None, ROWS_QS, HEAD_DIM), lambda b, pt: (b, 0, 0))
    n_spec = pl.BlockSpec((None, NEW_ROWS, HEAD_DIM), lambda b, pt: (b, 0, 0))
    any_spec = pl.BlockSpec(memory_space=pl.ANY)
    grid_spec = pltpu.PrefetchScalarGridSpec(
        num_scalar_prefetch=1,
        grid=(DEC_BATCH,),
        in_specs=[q_spec, n_spec, n_spec, any_spec, any_spec],
        out_specs=q_spec,
        scratch_shapes=[pltpu.VMEM((2, KEY_ROWS, HEAD_DIM), F32),
                        pltpu.VMEM((2, KEY_ROWS, HEAD_DIM), F32),
                        pltpu.VMEM((KEY_ROWS, ROWS_QS), F32),
                        pltpu.SemaphoreType.DMA((2,)),
                        pltpu.SemaphoreType.DMA((2,))],
    )
    return pl.pallas_call(
        _attn_sample_kernel,
        grid_spec=grid_spec,
        out_shape=jax.ShapeDtypeStruct((DEC_BATCH, ROWS_QS, HEAD_DIM), F32),
        compiler_params=_params(("arbitrary",)),
        name="attn_sample",
    )(page_table.reshape(-1), q_s, k_new, v_new, cache_k, cache_v)


def _merge_kernel(ap_ref, as_ref, tp_ref, ts_ref, wc_ref, wa_ref, gc_ref, ga_ref, o_ref, wcb_ref, wab_ref):
    i = pl.program_id(1)

    @pl.when(i == 0)
    def _():
        wcb_ref[...] = wc_ref[...].astype(BF16)
        wab_ref[...] = wa_ref[...].astype(BF16)

    def emit(a_ref, t_ref):
        ya = jnp.dot(a_ref[...], wcb_ref[...], preferred_element_type=F32)
        yb = jnp.dot(t_ref[...], wab_ref[...], preferred_element_type=F32)
        o_ref[...] = (_sigmoid(gc_ref[...]) * ya + _sigmoid(ga_ref[...]) * yb).astype(BF16)

    @pl.when(i < N_PT)
    def _():
        emit(ap_ref, tp_ref)

    @pl.when(i == N_PT)
    def _():
        emit(as_ref, ts_ref)


def _merge(a_p, a_s, attn_p, attn_s, w_conv_branch, w_attn_branch, h):
    act_p = pl.BlockSpec((TILE, D_CONV), lambda j, i: (jnp.minimum(i, N_PT - 1), 0))
    act_s = pl.BlockSpec((TILE, D_CONV), lambda j, i: (0, 0))
    wsp = pl.BlockSpec((D_CONV, TILE), lambda j, i: (0, j))

    def gate(col):
        return pl.BlockSpec((TILE, TILE), lambda j, i, col=col: (i, col // TILE + j))

    return pl.pallas_call(
        _merge_kernel,
        grid=(D_MODEL // TILE, N_TOK // TILE),
        in_specs=[act_p, act_s, act_p, act_s, wsp, wsp, gate(COL_GC), gate(COL_GA)],
        out_specs=pl.BlockSpec((TILE, TILE), lambda j, i: (i, j)),
        out_shape=jax.ShapeDtypeStruct((N_TOK, D_MODEL), BF16),
        scratch_shapes=[pltpu.VMEM((D_CONV, TILE), BF16), pltpu.VMEM((D_ATTN, TILE), BF16)],
        compiler_params=_params(("arbitrary", "arbitrary")),
        name="merge",
    )(a_p, a_s, attn_p, attn_s, w_conv_branch, w_attn_branch, h, h)


def _out_proj_kernel(m_ref, w_ref, xp_ref, xs_ref, o_ref, wbf_ref):
    i = pl.program_id(1)

    @pl.when(i == 0)
    def _():
        wbf_ref[...] = w_ref[...].astype(BF16)

    y = jnp.dot(m_ref[...], wbf_ref[...], preferred_element_type=F32)

    @pl.when(i < N_PT)
    def _():
        o_ref[...] = ALPHA * xp_ref[...] + y

    @pl.when(i == N_PT)
    def _():
        o_ref[...] = ALPHA * xs_ref[...] + y


def _out_proj(merged, w_o, x_p, x_s):
    return pl.pallas_call(
        _out_proj_kernel,
        grid=(D_MODEL // TILE, N_TOK // TILE),
        in_specs=[pl.BlockSpec((TILE, D_MODEL), lambda j, i: (i, 0)),
                  pl.BlockSpec((D_MODEL, TILE), lambda j, i: (0, j)),
                  pl.BlockSpec((TILE, TILE), lambda j, i: (jnp.minimum(i, N_PT - 1), j)),
                  pl.BlockSpec((TILE, TILE), lambda j, i: (0, j))],
        out_specs=pl.BlockSpec((TILE, TILE), lambda j, i: (i, j)),
        out_shape=jax.ShapeDtypeStruct((N_TOK, D_MODEL), F32),
        scratch_shapes=[pltpu.VMEM((D_MODEL, TILE), BF16)],
        compiler_params=_params(("arbitrary", "arbitrary")),
        name="out_proj",
    )(merged, w_o, x_p, x_s)


def _layer_norm(x, g, b):
    mu = jnp.mean(x, axis=-1, keepdims=True)
    xc = x - mu
    var = jnp.mean(xc * xc, axis=-1, keepdims=True)
    return xc * lax.rsqrt(var + LN_EPS) * g + b


def _rows_to_chunks(x, dst_ref, n_rows, first_chunk=0):
    for c in range(x.shape[1] // LANES):
        dst_ref[pl.ds(first_chunk + c, n_rows, stride=ROW_PITCH), :] = x[:, c * LANES:(c + 1) * LANES]


def _pad_chunks(dst_ref, n_rows):
    zeros = jnp.zeros((n_rows, LANES), F32)
    for c in range(ROW_CHUNKS, ROW_PITCH):
        dst_ref[pl.ds(c, n_rows, stride=ROW_PITCH), :] = zeros


def _chunks_to_rows(src_ref, n_rows):
    return [src_ref[pl.ds(c, n_rows, stride=ROW_PITCH), :] for c in range(ROW_CHUNKS)]


GATHER_UNROLL = 8


def _chunk_row_copy(src_hbm, src_row, buf_ref, dst_row, sem):
    src = pl.multiple_of(src_row * ROW_PITCH, SUBLANES)
    dst = pl.multiple_of(dst_row * ROW_PITCH, SUBLANES)
    return pltpu.make_async_copy(src_hbm.at[pl.ds(src, ROW_CHUNKS)], buf_ref.at[pl.ds(dst, ROW_CHUNKS)], sem)


def _ln_router_kernel(x_ref, g_ref, b_ref, wr_ref, br_ref, h_ref, hc_ref, e_ref, w_ref):
    h = _layer_norm(x_ref[...], g_ref[...], b_ref[...])
    h_ref[...] = h
    _rows_to_chunks(h, hc_ref, LN_TM)
    _pad_chunks(hc_ref, LN_TM)
    x = jnp.dot(h, wr_ref[...], precision=lax.Precision.HIGHEST, preferred_element_type=F32) + br_ref[...]
    lane = lax.broadcasted_iota(jnp.int32, x.shape, 1)
    lane_f = lane.astype(F32)
    ninf = -jnp.inf

    def first_lane(hit):
        return jnp.min(jnp.where(hit, lane_f, float(LANES)), axis=-1, keepdims=True)

    is_g = lane < N_GROUPS
    glog = jnp.where(is_g, x, ninf)
    gmax = jnp.max(glog, axis=-1, keepdims=True)
    grp = first_lane(glog == gmax)
    wg = 1.0 / jnp.sum(jnp.where(is_g, jnp.exp(x - gmax), 0.0), axis=-1, keepdims=True)
    lane_grp = lax.shift_right_logical(lane, 3).astype(F32)
    in_grp = (lane >= N_GROUPS) & (lane < N_GROUPS + N_EXPERTS) & (lane_grp == grp + 1.0)
    elog = jnp.where(in_grp, x, ninf)
    t1 = jnp.max(elog, axis=-1, keepdims=True)
    i1 = first_lane(elog == t1)
    elog2 = jnp.where(lane_f == i1, ninf, elog)
    t2 = jnp.max(elog2, axis=-1, keepdims=True)
    i2 = first_lane(elog2 == t2)
    e2 = jnp.exp(t2 - t1)
    den = 1.0 + e2
    e_ref[...] = jnp.where(lane == 0, i1 - N_GROUPS, jnp.where(lane == 1, i2 - N_GROUPS, 0.0)).astype(jnp.int32)
    w_ref[...] = jnp.where(lane == 0, wg * (1.0 / den), jnp.where(lane == 1, wg * (e2 / den), 0.0))


def _ln_router(pre, g, b, w_router, b_router):
    tm = LN_TM
    row = pl.BlockSpec((tm, D_MODEL), lambda i: (i, 0))
    vec = pl.BlockSpec((1, D_MODEL), lambda i: (0, 0))
    small = pl.BlockSpec((tm, LANES), lambda i: (i, 0))
    return pl.pallas_call(
        _ln_router_kernel,
        grid=(N_TOK // tm,),
        in_specs=[row, vec, vec,
                  pl.BlockSpec((D_MODEL, LANES), lambda i: (0, 0)),
                  pl.BlockSpec((1, LANES), lambda i: (0, 0))],
        out_specs=[row, pl.BlockSpec((tm * ROW_PITCH, LANES), lambda i: (i, 0)), small, small],
        out_shape=[jax.ShapeDtypeStruct((N_TOK, D_MODEL), F32),
                   jax.ShapeDtypeStruct((N_TOK * ROW_PITCH, LANES), F32),
                   jax.ShapeDtypeStruct((N_TOK, LANES), jnp.int32),
                   jax.ShapeDtypeStruct((N_TOK, LANES), F32)],
        compiler_params=_params(("arbitrary",)),
        name="ln_router",
    )(pre, g, b, w_router, b_router)


BUF_ROWS = MOE_TM * ROW_PITCH


def _block_rows(src_ref, cnt_ref, tok_ref, hc_hbm, buf_ref, sem, blk, slot):
    cnt = cnt_ref[blk]
    base = src_ref[blk]
    n_groups = (cnt + GATHER_UNROLL - 1) // GATHER_UNROLL
    slot_ref = buf_ref.at[pl.ds(pl.multiple_of(slot * BUF_ROWS, SUBLANES), BUF_ROWS)]

    def start():
        def issue(g, c):
            for u in range(GATHER_UNROLL):
                r = g * GATHER_UNROLL + u
                tok = tok_ref[base + jnp.minimum(r, cnt - 1)]
                _chunk_row_copy(hc_hbm, tok, slot_ref, r, sem.at[slot]).start(priority=u % 2)
            return c

        lax.fori_loop(0, n_groups, issue, 0)

    def wait():
        def one(g, c):
            n = GATHER_UNROLL * ROW_CHUNKS
            pltpu.make_async_copy(hc_hbm.at[pl.ds(0, n)], slot_ref.at[pl.ds(0, n)], sem.at[slot]).wait()
            return c

        lax.fori_loop(0, n_groups, one, 0)

    return start, wait


def _mlp_up_kernel(nused_ref, e_ref, src_ref, cnt_ref, tok_ref, hc_hbm, wg_ref, wu_ref, o_ref, buf_ref, x_ref, sem):
    del e_ref
    b = pl.program_id(0)
    slot = b % 2

    @pl.when(pl.program_id(1) == 0)
    def _():
        rows = lambda blk, s: _block_rows(src_ref, cnt_ref, tok_ref, hc_hbm, buf_ref, sem, blk, s)

        @pl.when(b == 0)
        def _():
            buf_ref[...] = jnp.zeros_like(buf_ref)
            rows(0, 0)[0]()

        @pl.when(b + 1 < nused_ref[0])
        def _():
            rows(b + 1, 1 - slot)[0]()

        rows(b, slot)[1]()
        live = lax.broadcasted_iota(jnp.int32, (MOE_TM, LANES), 0) < cnt_ref[b]
        first = pl.multiple_of(slot * BUF_ROWS, SUBLANES)
        for c in range(ROW_CHUNKS):
            chunk = buf_ref[pl.ds(first + c, MOE_TM, stride=ROW_PITCH), :]
            x_ref[:, c * LANES:(c + 1) * LANES] = jnp.where(live, chunk, 0.0).astype(BF16)

    x = x_ref[...]
    g = jnp.dot(x, wg_ref[...].astype(BF16), preferred_element_type=F32)
    u = jnp.dot(x, wu_ref[...].astype(BF16), preferred_element_type=F32)
    o_ref[...] = (g * _sigmoid(g) * u).astype(BF16)


def _mlp_up(n_used, blk_e, blk_src, blk_cnt, sorted_tok, h_chunks, w_gate, w_up):
    wsp = pl.BlockSpec((None, D_MODEL, MOE_TF), lambda b, f, nu, e, src, cnt: (e[b], 0, f))
    grid_spec = pltpu.PrefetchScalarGridSpec(
        num_scalar_prefetch=4,
        grid=(n_used[0], D_EXPERT // MOE_TF),
        in_specs=[pl.BlockSpec(memory_space=pltpu.SMEM), pl.BlockSpec(memory_space=pl.ANY), wsp, wsp],
        out_specs=pl.BlockSpec((MOE_TM, MOE_TF), lambda b, f, nu, e, src, cnt: (b, f)),
        scratch_shapes=[pltpu.VMEM((2 * BUF_ROWS, LANES), F32), pltpu.VMEM((MOE_TM, D_MODEL), BF16),
                        pltpu.SemaphoreType.DMA((2,))],
    )
    return pl.pallas_call(
        _mlp_up_kernel,
        grid_spec=grid_spec,
        out_shape=jax.ShapeDtypeStruct((MOE_P, D_EXPERT), BF16),
        compiler_params=_params(("arbitrary", "arbitrary")),
        name="moe_up",
    )(n_used, blk_e, blk_src, blk_cnt, sorted_tok, h_chunks, w_gate, w_up)


def _mlp_down_kernel(nused_ref, e_ref, x_ref, wd_ref, o_ref):
    del nused_ref, e_ref
    o_ref[...] = jnp.dot(x_ref[...], wd_ref[...].astype(BF16), preferred_element_type=F32)


def _mlp_down(n_used, blk_e, hmid, w_down):
    grid_spec = pltpu.PrefetchScalarGridSpec(
        num_scalar_prefetch=2,
        grid=(n_used[0], D_MODEL // MOE_TN),
        in_specs=[pl.BlockSpec((MOE_TM, D_EXPERT), lambda b, n, nu, e: (b, 0)),
                  pl.BlockSpec((None, D_EXPERT, MOE_TN), lambda b, n, nu, e: (e[b], 0, n))],
        out_specs=pl.BlockSpec((MOE_TM, MOE_TN), lambda b, n, nu, e: (b, n)),
    )
    return pl.pallas_call(
        _mlp_down_kernel,
        grid_spec=grid_spec,
        out_shape=jax.ShapeDtypeStruct((MOE_P, D_MODEL), F32),
        compiler_params=_params(("arbitrary", "arbitrary")),
        name="moe_down",
    )(n_used, blk_e, hmid, w_down)


def _combine_kernel(dest_ref, w_ref, h_ref, g_ref, b_ref, ys_hbm, zp_ref, zs_ref, buf_ref, sem):
    i = pl.program_id(0)

    def row_copy(src_row, k, r):
        return pltpu.make_async_copy(ys_hbm.at[pl.ds(src_row, 1)], buf_ref.at[k, pl.ds(r, 1)], sem)

    def issue(g, c):
        for u in range(GATHER_UNROLL):
            r = g * GATHER_UNROLL + u
            for k in range(TOPK_IN_GROUP):
                row_copy(dest_ref[0, TOPK_IN_GROUP * r + k], k, r).start(priority=(u + k) % 2)
        return c

    lax.fori_loop(0, LN_TM // GATHER_UNROLL, issue, 0)
    for k in range(TOPK_IN_GROUP):
        pltpu.make_async_copy(ys_hbm.at[pl.ds(0, LN_TM)], buf_ref.at[k], sem).wait()
    w = w_ref[...]
    moe = buf_ref[0] * w[:, 0:1] + buf_ref[1] * w[:, 1:2]
    z = _layer_norm(ALPHA * h_ref[...] + moe, g_ref[...], b_ref[...])

    @pl.when(i < LN_PT)
    def _():
        zp_ref[...] = z

    @pl.when(i >= LN_PT)
    def _():
        zs_ref[...] = z


def _combine(dest, wts, h1, g, b, ys):
    tm = LN_TM
    row = pl.BlockSpec((tm, D_MODEL), lambda i: (i, 0))
    vec = pl.BlockSpec((1, D_MODEL), lambda i: (0, 0))
    zp, zs = _split_specs(tm, D_MODEL, LN_PT)
    return pl.pallas_call(
        _combine_kernel,
        grid=(N_TOK // tm,),
        in_specs=[pl.BlockSpec((None, 1, TOPK_IN_GROUP * tm), lambda i: (i, 0, 0), memory_space=pltpu.SMEM),
                  pl.BlockSpec((tm, LANES), lambda i: (i, 0)),
                  row, vec, vec,
                  pl.BlockSpec(memory_space=pl.ANY)],
        out_specs=[zp, zs],
        out_shape=[jax.ShapeDtypeStruct((N_P, D_MODEL), F32), jax.ShapeDtypeStruct((N_S, D_MODEL), F32)],
        scratch_shapes=[pltpu.VMEM((TOPK_IN_GROUP, tm, D_MODEL), F32), pltpu.SemaphoreType.DMA(())],
        compiler_params=_params(("arbitrary",)),
        name="moe_combine",
    )(dest.reshape(N_TOK // tm, 1, TOPK_IN_GROUP * tm), wts, h1, g, b, ys)


def _moe_tables(eid):
    flat_e = eid.reshape(-1)
    experts = jnp.arange(N_EXPERTS, dtype=jnp.int32)
    iota_a = jnp.arange(N_ASSIGN, dtype=jnp.int32)
    se, order = lax.sort((flat_e, iota_a), num_keys=1)
    counts = jnp.sum((flat_e[:, None] == experts[None, :]).astype(jnp.int32), axis=0)
    nblk = (counts + MOE_TM - 1) // MOE_TM
    blk_end = jnp.cumsum(nblk)
    blk_start = blk_end - nblk
    start = jnp.cumsum(counts) - counts
    n_used = blk_end[-1]
    shift = blk_start * MOE_TM - start
    dest_sorted = iota_a + jnp.sum(jnp.where(se[:, None] == experts[None, :], shift[None, :], 0), axis=1)
    _, dest = lax.sort((order, dest_sorted), num_keys=1)
    blks = jnp.arange(MOE_NB, dtype=jnp.int32)
    b_e = jnp.minimum(jnp.sum((blk_end[None, :] <= blks[:, None]).astype(jnp.int32), axis=1), N_EXPERTS - 1)
    b_first = (blks - blk_start[b_e]) * MOE_TM
    blk_src = jnp.clip(start[b_e] + b_first, 0, N_ASSIGN - 1)
    blk_cnt = jnp.clip(counts[b_e] - b_first, 1, MOE_TM)
    sorted_tok = order // TOPK_IN_GROUP

    block_tables = (n_used.reshape(1).astype(jnp.int32), b_e.astype(jnp.int32))
    gather_tables = (blk_src.astype(jnp.int32), blk_cnt.astype(jnp.int32), sorted_tok.astype(jnp.int32))
    return block_tables, gather_tables, dest.astype(jnp.int32)


def kernel(x_prompt, x_sample, cache_k, cache_v, state_conv, page_table, w_in, conv_w, w_conv_branch, w_attn_branch, w_o, ln1_g, ln1_b, w_router_group, b_router_group, w_router_expert, b_router_expert, w_gate, w_up, w_down, ln2_g, ln2_b):
    l = 0
    x_p = x_prompt.reshape(N_P, D_MODEL)
    x_s = x_sample.transpose(1, 0, 2).reshape(N_S, D_MODEL)
    h = _in_proj(x_p.astype(BF16), x_s.astype(BF16), w_in[l])

    a_p, conv_p = _conv_prompt(h, conv_w[l])
    a_s, conv_s_t = _conv_sample(h, state_conv[l].transpose(1, 0, 2), conv_w[l])

    q_p, k_p, v_p, q_s, k_s, v_s = _rope(h, *_rope_tables())
    attn_p = _attn_prompt(q_p, k_p, v_p)

    def seq_major(a, width):
        return a.reshape(DEC_SEQ, DEC_BATCH, width // HEAD_DIM, HEAD_DIM).transpose(1, 0, 2, 3)

    k_s_b, v_s_b = seq_major(k_s, D_KV), seq_major(v_s, D_KV)
    q_s_b = seq_major(q_s, D_ATTN).transpose(0, 2, 1, 3).reshape(DEC_BATCH, ROWS_QS, HEAD_DIM)
    pool_rows = cache_k.shape[1] * PAGE_ROWS
    o_s = _attn_sample(page_table, q_s_b,
                       k_s_b.reshape(DEC_BATCH, NEW_ROWS, HEAD_DIM), v_s_b.reshape(DEC_BATCH, NEW_ROWS, HEAD_DIM),
                       cache_k[l].reshape(pool_rows, HEAD_DIM), cache_v[l].reshape(pool_rows, HEAD_DIM))
    attn_s = (o_s.reshape(DEC_BATCH, N_HEADS, DEC_SEQ, HEAD_DIM).transpose(2, 0, 1, 3)
              .reshape(N_S, D_ATTN).astype(BF16))

    merged = _merge(a_p, a_s, attn_p, attn_s, w_conv_branch[l], w_attn_branch[l], h)
    pre = _out_proj(merged, w_o[l], x_p, x_s)

    w_router = jnp.pad(jnp.concatenate([w_router_group[l], w_router_expert[l]], axis=1),
                       ((0, 0), (0, LANES - N_GROUPS - N_EXPERTS)))
    b_router = jnp.pad(jnp.concatenate([b_router_group[l], b_router_expert[l]]),
                       (0, LANES - N_GROUPS - N_EXPERTS)).reshape(1, LANES)
    h1, h_chunks, eid, wts = _ln_router(pre, ln1_g[l].reshape(1, D_MODEL), ln1_b[l].reshape(1, D_MODEL),
                                        w_router, b_router)
    block_tables, gather_tables, dest = _moe_tables(eid[:, :TOPK_IN_GROUP])
    hmid = _mlp_up(*block_tables, *gather_tables, h_chunks, w_gate[l], w_up[l])
    ys = _mlp_down(*block_tables, hmid, w_down[l])
    z_p, z_s = _combine(dest, wts, h1, ln2_g[l].reshape(1, D_MODEL), ln2_b[l].reshape(1, D_MODEL), ys)

    y_prompt = z_p.reshape(BATCH, SEQ, D_MODEL)
    y_sample = z_s.reshape(DEC_SEQ, DEC_BATCH, D_MODEL).transpose(1, 0, 2)
    k_prompt = k_p.reshape(1, BATCH, SEQ, N_KV_HEADS, HEAD_DIM)
    v_prompt = v_p.reshape(1, BATCH, SEQ, N_KV_HEADS, HEAD_DIM)
    conv_prompt = conv_p[None]
    conv_sample = conv_s_t.transpose(1, 0, 2)[None]
    return (y_prompt, y_sample, k_prompt, v_prompt, conv_prompt, k_s_b[None], v_s_b[None], conv_sample)
```

```python
import jax
import jax.numpy as jnp
from jax import lax
from jax.experimental import pallas as pl
from jax.experimental.pallas import tpu as pltpu

D_MODEL = 4096
BATCH = 4
SEQ = 2048
DEC_BATCH = 128
DEC_SEQ = 4
PAST_LEN = 2048
PAGE_SIZE = 128
N_PAGES = PAST_LEN // PAGE_SIZE
D_CONV = D_MODEL // 2
CONV_WIDTH = 3
HEAD_DIM = 128
N_HEADS = 16
N_KV_HEADS = 4
GROUP = N_HEADS // N_KV_HEADS
D_ATTN = N_HEADS * HEAD_DIM
D_KV = N_KV_HEADS * HEAD_DIM
ROT_DIM = HEAD_DIM // 4
ROPE_THETA = 500000.0
MOBA_BLOCK = 256
MOBA_TOPK = 3
N_GROUPS = 8
EXPERTS_PER_GROUP = 8
N_EXPERTS = N_GROUPS * EXPERTS_PER_GROUP
TOPK_IN_GROUP = 2
D_EXPERT = D_MODEL // 4
DEPTH = 1
ALPHA = (2 * DEPTH) ** 0.25
LN_EPS = 1e-5
D_IN_TOTAL = 3 * D_CONV + D_ATTN + 2 * D_KV + 2 * D_MODEL

N_P = BATCH * SEQ
N_S = DEC_BATCH * DEC_SEQ
N_TOK = N_P + N_S
N_ASSIGN = N_TOK * TOPK_IN_GROUP

COL_CB, COL_CC, COL_CX = 0, D_CONV, 2 * D_CONV
COL_Q = 3 * D_CONV
COL_K = COL_Q + D_ATTN
COL_V = COL_K + D_KV
COL_GC = COL_V + D_KV
COL_GA = COL_GC + D_MODEL

LANES = 128
SUBLANES = 8
TILE = 512
N_PT = N_P // TILE
LN_TM = 256
LN_PT = N_P // LN_TM
MOE_TM = 384
MOE_NB = -(-N_ASSIGN // MOE_TM) + N_EXPERTS
MOE_P = MOE_NB * MOE_TM
MOE_TF = 256
MOE_TN = 2048
ROW_CHUNKS = D_MODEL // LANES
ROW_PITCH = 40
VMEM_LIMIT = 56 * 1024 * 1024

BF16 = jnp.bfloat16
F32 = jnp.float32
_NT = (((1,), (1,)), ((), ()))
_TN = (((0,), (0,)), ((), ()))


def _params(sem, vmem=VMEM_LIMIT):
    return pltpu.CompilerParams(dimension_semantics=sem, vmem_limit_bytes=vmem)


def _sigmoid(x):
    return 1.0 / (1.0 + jnp.exp(-x))


IN_TN = 1024


def _in_proj_kernel(x_ref, w_ref, o_ref, wbf_ref):
    @pl.when(pl.program_id(1) == 0)
    def _():
        wbf_ref[...] = w_ref[...].astype(BF16)

    o_ref[...] = jnp.dot(x_ref[...], wbf_ref[...], preferred_element_type=F32)


def _in_proj(x_bf, w_in):
    return pl.pallas_call(
        _in_proj_kernel,
        grid=(D_IN_TOTAL // IN_TN, N_TOK // TILE),
        in_specs=[pl.BlockSpec((TILE, D_MODEL), lambda j, i: (i, 0)),
                  pl.BlockSpec((D_MODEL, IN_TN), lambda j, i: (0, j))],
        out_specs=pl.BlockSpec((TILE, IN_TN), lambda j, i: (i, j)),
        out_shape=jax.ShapeDtypeStruct((N_TOK, D_IN_TOTAL), F32),
        scratch_shapes=[pltpu.VMEM((D_MODEL, IN_TN), BF16)],
        compiler_params=_params(("arbitrary", "arbitrary"), vmem=60 * 1024 * 1024),
        name="in_proj",
    )(x_bf, w_in)


def _conv_prompt_kernel(cb_ref, cc_ref, cx_ref, w_ref, a_ref, tail_ref, prev_ref):
    @pl.when(pl.program_id(2) == 0)
    def _():
        prev_ref[...] = jnp.zeros_like(prev_ref)

    u = cc_ref[...] * cx_ref[...]
    rows = lax.broadcasted_iota(jnp.int32, u.shape, 0)
    p0 = prev_ref[0:1, :]
    p1 = prev_ref[1:2, :]
    um1 = jnp.where(rows == 0, p1, pltpu.roll(u, 1, 0))
    um2 = jnp.where(rows == 0, p0, jnp.where(rows == 1, p1, pltpu.roll(u, 2, 0)))
    y = w_ref[0:1, :] * um2 + w_ref[1:2, :] * um1 + w_ref[2:3, :] * u
    a_ref[...] = (cb_ref[...] * y).astype(BF16)
    last = u[TILE - 2:TILE, :]
    prev_ref[0:2, :] = last
    tail_ref[...] = last


def _conv_prompt(h, conv_w):
    nr = SEQ // TILE
    nc = D_CONV // TILE

    def sec(col):
        return pl.BlockSpec((TILE, TILE), lambda b, c, r, col=col: (b * nr + r, col // TILE + c))

    return pl.pallas_call(
        _conv_prompt_kernel,
        grid=(BATCH, nc, nr),
        in_specs=[sec(COL_CB), sec(COL_CC), sec(COL_CX),
                  pl.BlockSpec((CONV_WIDTH, TILE), lambda b, c, r: (0, c))],
        out_specs=[pl.BlockSpec((TILE, TILE), lambda b, c, r: (b * nr + r, c)),
                   pl.BlockSpec((None, CONV_WIDTH - 1, TILE), lambda b, c, r: (b, 0, c))],
        out_shape=[jax.ShapeDtypeStruct((N_P, D_CONV), BF16),
                   jax.ShapeDtypeStruct((BATCH, CONV_WIDTH - 1, D_CONV), F32)],
        scratch_shapes=[pltpu.VMEM((SUBLANES, TILE), F32)],
        compiler_params=_params(("arbitrary", "arbitrary", "arbitrary")),
        name="conv_prompt",
    )(h, h, h, conv_w)


def _conv_sample_kernel(cb_ref, cc_ref, cx_ref, st_ref, w_ref, a_ref, tail_ref):
    b = DEC_BATCH
    w0, w1, w2 = w_ref[0:1, :], w_ref[1:2, :], w_ref[2:3, :]
    u = cc_ref[...] * cx_ref[...]
    up = [st_ref[0], st_ref[1]] + [u[t * b:(t + 1) * b, :] for t in range(DEC_SEQ)]
    for t in range(DEC_SEQ):
        y = w0 * up[t] + w1 * up[t + 1] + w2 * up[t + 2]
        a_ref[t * b:(t + 1) * b, :] = (cb_ref[t * b:(t + 1) * b, :] * y).astype(BF16)
    tail_ref[0] = up[DEC_SEQ]
    tail_ref[1] = up[DEC_SEQ + 1]


def _conv_sample(h, state_t, conv_w):
    nc = D_CONV // TILE

    def sec(col):
        return pl.BlockSpec((N_S, TILE), lambda c, col=col: (N_PT, col // TILE + c))

    return pl.pallas_call(
        _conv_sample_kernel,
        grid=(nc,),
        in_specs=[sec(COL_CB), sec(COL_CC), sec(COL_CX),
                  pl.BlockSpec((CONV_WIDTH - 1, DEC_BATCH, TILE), lambda c: (0, 0, c)),
                  pl.BlockSpec((CONV_WIDTH, TILE), lambda c: (0, c))],
        out_specs=[pl.BlockSpec((N_S, TILE), lambda c: (0, c)),
                   pl.BlockSpec((CONV_WIDTH - 1, DEC_BATCH, TILE), lambda c: (0, 0, c))],
        out_shape=[jax.ShapeDtypeStruct((N_S, D_CONV), BF16),
                   jax.ShapeDtypeStruct((CONV_WIDTH - 1, DEC_BATCH, D_CONV), F32)],
        compiler_params=_params(("arbitrary",)),
        name="conv_sample",
    )(h, h, h, state_t, conv_w)


def _rope_kernel(q_ref, k_ref, v_ref, c_ref, s1_ref, s2_ref, qp_ref, kp_ref, vp_ref, qs_ref, ks_ref, vs_ref):
    c, s1, s2 = c_ref[...], s1_ref[...], s2_ref[...]

    def rot(x):
        return x * c + pltpu.roll(x, LANES - ROT_DIM // 2, 1) * s1 + pltpu.roll(x, ROT_DIM // 2, 1) * s2

    def emit(qo_ref, ko_ref, vo_ref):
        for hd in range(N_HEADS):
            sl = slice(hd * HEAD_DIM, (hd + 1) * HEAD_DIM)
            qo_ref[:, sl] = rot(q_ref[:, sl])
        for hd in range(N_KV_HEADS):
            sl = slice(hd * HEAD_DIM, (hd + 1) * HEAD_DIM)
            ko_ref[:, sl] = rot(k_ref[:, sl])
        vo_ref[...] = v_ref[...]

    @pl.when(pl.program_id(0) < N_PT)
    def _():
        emit(qp_ref, kp_ref, vp_ref)

    @pl.when(pl.program_id(0) == N_PT)
    def _():
        emit(qs_ref, ks_ref, vs_ref)


def _split_specs(tm, width, n_prompt_tiles):
    return (pl.BlockSpec((tm, width), lambda i: (jnp.minimum(i, n_prompt_tiles - 1), 0)),
            pl.BlockSpec((tm, width), lambda i: (jnp.maximum(i - n_prompt_tiles, 0), 0)))


def _rope(h, tab_c, tab_s1, tab_s2):
    tiles_per_seq = SEQ // TILE

    def tab_map(i):
        return (jnp.where(i < N_PT, i % tiles_per_seq, tiles_per_seq), 0)

    tab = pl.BlockSpec((TILE, LANES), tab_map)
    qp, qs = _split_specs(TILE, D_ATTN, N_PT)
    kp, ks = _split_specs(TILE, D_KV, N_PT)
    return pl.pallas_call(
        _rope_kernel,
        grid=(N_TOK // TILE,),
        in_specs=[pl.BlockSpec((TILE, D_ATTN), lambda i: (i, COL_Q // D_ATTN)),
                  pl.BlockSpec((TILE, D_KV), lambda i: (i, COL_K // D_KV)),
                  pl.BlockSpec((TILE, D_KV), lambda i: (i, COL_V // D_KV)),
                  tab, tab, tab],
        out_specs=[qp, kp, kp, qs, ks, ks],
        out_shape=[jax.ShapeDtypeStruct((N_P, D_ATTN), F32),
                   jax.ShapeDtypeStruct((N_P, D_KV), F32),
                   jax.ShapeDtypeStruct((N_P, D_KV), F32),
                   jax.ShapeDtypeStruct((N_S, D_ATTN), F32),
                   jax.ShapeDtypeStruct((N_S, D_KV), F32),
                   jax.ShapeDtypeStruct((N_S, D_KV), F32)],
        compiler_params=_params(("arbitrary",)),
        name="rope",
    )(h, h, h, tab_c, tab_s1, tab_s2)


def _rope_tables():
    half = ROT_DIM // 2
    inv = ROPE_THETA ** (-jnp.arange(half, dtype=F32) / half)
    pos = jnp.concatenate([jnp.arange(SEQ, dtype=jnp.int32),
                           PAST_LEN + jnp.repeat(jnp.arange(DEC_SEQ, dtype=jnp.int32), DEC_BATCH)])
    ang = pos.astype(F32)[:, None] * inv[None, :]
    cos, sin = jnp.cos(ang), jnp.sin(ang)
    n = pos.shape[0]
    ones = jnp.ones((n, HEAD_DIM - ROT_DIM), F32)
    zeros = jnp.zeros((n, HEAD_DIM - ROT_DIM), F32)
    zh = jnp.zeros((n, half), F32)
    tab_c = jnp.concatenate([cos, cos, ones], 1)
    tab_s1 = jnp.concatenate([-sin, zh, zeros], 1)
    tab_s2 = jnp.concatenate([zh, sin, zeros], 1)
    return tab_c, tab_s1, tab_s2


ROWS_QP = GROUP * MOBA_BLOCK


def _attn_prompt_kernel(q_ref, k_ref, v_ref, o_ref, kmean_ref, kbf_ref, vt_ref):
    nb = SEQ // MOBA_BLOCK
    qi = pl.program_id(2)

    @pl.when(qi == 0)
    def _():
        k = k_ref[...]
        kmean_ref[...] = jnp.mean(k.reshape(nb, MOBA_BLOCK, HEAD_DIM), axis=1)
        for n in range(nb):
            rows = slice(n * MOBA_BLOCK, (n + 1) * MOBA_BLOCK)
            kbf_ref[n] = k[rows, :].astype(BF16)
            vt_ref[n] = jnp.transpose(v_ref[rows, :]).astype(BF16)

    q = jnp.concatenate([q_ref[:, g * HEAD_DIM:(g + 1) * HEAD_DIM] for g in range(GROUP)], axis=0)
    gate = lax.dot_general(kmean_ref[...], q, _NT, precision=lax.Precision.HIGHEST, preferred_element_type=F32)
    blk = lax.broadcasted_iota(jnp.int32, gate.shape, 0)
    past = blk < qi
    gate = jnp.where(past, gate, -jnp.inf)
    rank = jnp.zeros(gate.shape, jnp.int32)
    for m in range(nb):
        gm = gate[m:m + 1, :]
        beats = (gm > gate) | ((gm == gate) & (blk > m))
        rank = rank + beats.astype(jnp.int32)
    chosen_blocks = jnp.where((rank < MOBA_TOPK) & past, 1.0, 0.0)

    qs = (q * (HEAD_DIM ** -0.5)).astype(BF16)
    s = lax.dot_general(kbf_ref[qi], qs, _NT, preferred_element_type=F32)
    q_off = lax.broadcasted_iota(jnp.int32, s.shape, 1) & (MOBA_BLOCK - 1)
    s = jnp.where(lax.broadcasted_iota(jnp.int32, s.shape, 0) <= q_off, s, -jnp.inf)
    m0 = jnp.max(s, axis=0, keepdims=True)
    p = jnp.exp(s - m0)
    l0 = jnp.sum(p, axis=0, keepdims=True)
    acc0 = jnp.dot(vt_ref[qi], p.astype(BF16), preferred_element_type=F32)

    def body(n, carry):
        m, l, acc = carry
        s = lax.dot_general(kbf_ref[n], qs, _NT, preferred_element_type=F32)
        chosen = jnp.max(jnp.where(blk == n, chosen_blocks, 0.0), axis=0, keepdims=True) > 0.0
        m_new = jnp.where(chosen, jnp.maximum(m, jnp.max(s, axis=0, keepdims=True)), m)
        p = jnp.where(chosen, jnp.exp(s - m_new), 0.0)
        a = jnp.exp(m - m_new)
        l = a * l + jnp.sum(p, axis=0, keepdims=True)
        acc = a * acc + jnp.dot(vt_ref[n], p.astype(BF16), preferred_element_type=F32)
        return m_new, l, acc

    _, l, acc = lax.fori_loop(0, qi, body, (m0, l0, acc0))
    o = jnp.transpose(acc / l).astype(BF16)
    for g in range(GROUP):
        o_ref[:, g * HEAD_DIM:(g + 1) * HEAD_DIM] = o[g * MOBA_BLOCK:(g + 1) * MOBA_BLOCK, :]


def _attn_prompt(q_rot, k_rot, v):
    nq = SEQ // MOBA_BLOCK
    qw = GROUP * HEAD_DIM
    kv_spec = pl.BlockSpec((SEQ, HEAD_DIM), lambda b, kv, qi: (b, kv))
    return pl.pallas_call(
        _attn_prompt_kernel,
        grid=(BATCH, N_KV_HEADS, nq),
        in_specs=[pl.BlockSpec((MOBA_BLOCK, qw), lambda b, kv, qi: (b * nq + qi, kv)), kv_spec, kv_spec],
        out_specs=pl.BlockSpec((MOBA_BLOCK, qw), lambda b, kv, qi: (b * nq + qi, kv)),
        out_shape=jax.ShapeDtypeStruct((N_P, D_ATTN), BF16),
        scratch_shapes=[pltpu.VMEM((SEQ // MOBA_BLOCK, HEAD_DIM), F32),
                        pltpu.VMEM((SEQ // MOBA_BLOCK, MOBA_BLOCK, HEAD_DIM), BF16),
                        pltpu.VMEM((SEQ // MOBA_BLOCK, HEAD_DIM, MOBA_BLOCK), BF16)],
        compiler_params=_params(("arbitrary", "arbitrary", "arbitrary")),
        name="attn_prompt",
    )(q_rot, k_rot, v)


PAGE_ROWS = PAGE_SIZE * N_KV_HEADS
PAST_ROWS = N_PAGES * PAGE_ROWS
NEW_ROWS = DEC_SEQ * N_KV_HEADS
KEY_ROWS = PAST_ROWS + LANES
ROWS_QS = N_KV_HEADS * GROUP * DEC_SEQ
BLOCK_ROWS = MOBA_BLOCK * N_KV_HEADS


def _page_copies(cache_hbm, pt_ref, b, buf_ref, slot, sem):
    return [pltpu.make_async_copy(cache_hbm.at[pl.ds(pl.multiple_of(pt_ref[b * N_PAGES + p] * PAGE_ROWS, PAGE_ROWS),
                                                     PAGE_ROWS)],
                                  buf_ref.at[slot, pl.ds(p * PAGE_ROWS, PAGE_ROWS)], sem.at[slot])
            for p in range(N_PAGES)]


def _attn_sample_kernel(pt_ref, q_ref, kn_ref, vn_ref, ck_hbm, cv_hbm, o_ref, kbuf, vbuf, bias_ref, ksem, vsem):
    b = pl.program_id(0)
    nb = PAST_LEN // MOBA_BLOCK
    slot = b % 2

    @pl.when(b == 0)
    def _():
        for c in _page_copies(ck_hbm, pt_ref, 0, kbuf, 0, ksem) + _page_copies(cv_hbm, pt_ref, 0, vbuf, 0, vsem):
            c.start()
        zeros = jnp.zeros((KEY_ROWS - PAST_ROWS, HEAD_DIM), F32)
        for s in range(2):
            kbuf[s, PAST_ROWS:, :] = zeros
            vbuf[s, PAST_ROWS:, :] = zeros
        key = lax.broadcasted_iota(jnp.int32, (KEY_ROWS, ROWS_QS), 0)
        qrow = lax.broadcasted_iota(jnp.int32, (KEY_ROWS, ROWS_QS), 1)
        same_head = (key & (N_KV_HEADS - 1)) == lax.shift_right_logical(qrow, 4)
        new_t = lax.shift_right_arithmetic(key - PAST_ROWS, 2)
        ok = same_head & ((key < PAST_ROWS) | (new_t <= (qrow & (DEC_SEQ - 1))))
        bias_ref[...] = jnp.where(ok, 0.0, -jnp.inf)

    @pl.when(b + 1 < DEC_BATCH)
    def _():
        nxt = 1 - slot
        for c in (_page_copies(ck_hbm, pt_ref, b + 1, kbuf, nxt, ksem)
                  + _page_copies(cv_hbm, pt_ref, b + 1, vbuf, nxt, vsem)):
            c.start()

    for c in _page_copies(ck_hbm, pt_ref, b, kbuf, slot, ksem) + _page_copies(cv_hbm, pt_ref, b, vbuf, slot, vsem):
        c.wait()
    kbuf[slot, PAST_ROWS:PAST_ROWS + NEW_ROWS, :] = kn_ref[...]
    vbuf[slot, PAST_ROWS:PAST_ROWS + NEW_ROWS, :] = vn_ref[...]

    q = q_ref[...]
    k_all = kbuf[slot]
    ksum = jnp.sum(k_all[:PAST_ROWS].reshape(nb, BLOCK_ROWS // SUBLANES, SUBLANES, HEAD_DIM), axis=1)
    ksum = (ksum + pltpu.roll(ksum, N_KV_HEADS, 1)).reshape(nb * SUBLANES, HEAD_DIM)
    gate = lax.dot_general(ksum, q, _NT, precision=lax.Precision.HIGHEST, preferred_element_type=F32)
    grow = lax.broadcasted_iota(jnp.int32, gate.shape, 0)
    gq = lax.broadcasted_iota(jnp.int32, gate.shape, 1)
    mine = (grow & (SUBLANES - 1)) == lax.shift_right_logical(gq, 4)
    gblk = lax.shift_right_logical(grow, 3)
    gate = jnp.where(mine, gate, -jnp.inf)
    rank = jnp.zeros(gate.shape, jnp.int32)
    for n in range(nb):
        gn = jnp.max(jnp.where(gblk == n, gate, -jnp.inf), axis=0, keepdims=True)
        beats = (gn > gate) | ((gn == gate) & (gblk > n))
        rank = rank + beats.astype(jnp.int32)
    sel = mine & (rank < MOBA_TOPK)
    blk_bias = [jnp.where(jnp.max(jnp.where((gblk == n) & sel, 1.0, 0.0), axis=0, keepdims=True) > 0.0, 0.0, -jnp.inf)
                for n in range(nb)]

    qs = (q * (HEAD_DIM ** -0.5)).astype(BF16)
    s = lax.dot_general(k_all.astype(BF16), qs, _NT, preferred_element_type=F32) + bias_ref[...]
    parts = [s[n * BLOCK_ROWS:(n + 1) * BLOCK_ROWS] + blk_bias[n] for n in range(nb)] + [s[PAST_ROWS:]]
    s = jnp.concatenate(parts, axis=0)
    m = jnp.max(s, axis=0, keepdims=True)
    p = jnp.exp(s - m)
    l = jnp.sum(p, axis=0, keepdims=True)
    p = (p * (1.0 / l)).astype(BF16)
    o_ref[...] = lax.dot_general(p, vbuf[slot].astype(BF16), _TN, preferred_element_type=F32)


def _attn_sample(page_table, q_s, k_new, v_new, cache_k, cache_v):
    q_spec = pl.BlockSpec((None, ROWS_QS, HEAD_DIM), lambda b, pt: (b, 0, 0))
    n_spec = pl.BlockSpec((None, NEW_ROWS, HEAD_DIM), lambda b, pt: (b, 0, 0))
    any_spec = pl.BlockSpec(memory_space=pl.ANY)
    grid_spec = pltpu.PrefetchScalarGridSpec(
        num_scalar_prefetch=1,
        grid=(DEC_BATCH,),
        in_specs=[q_spec, n_spec, n_spec, any_spec, any_spec],
        out_specs=q_spec,
        scratch_shapes=[pltpu.VMEM((2, KEY_ROWS, HEAD_DIM), F32),
                        pltpu.VMEM((2, KEY_ROWS, HEAD_DIM), F32),
                        pltpu.VMEM((KEY_ROWS, ROWS_QS), F32),
                        pltpu.SemaphoreType.DMA((2,)),
                        pltpu.SemaphoreType.DMA((2,))],
    )
    return pl.pallas_call(
        _attn_sample_kernel,
        grid_spec=grid_spec,
        out_shape=jax.ShapeDtypeStruct((DEC_BATCH, ROWS_QS, HEAD_DIM), F32),
        compiler_params=_params(("arbitrary",)),
        name="attn_sample",
    )(page_table.reshape(-1), q_s, k_new, v_new, cache_k, cache_v)


def _merge_kernel(ap_ref, as_ref, tp_ref, ts_ref, wc_ref, wa_ref, gc_ref, ga_ref, o_ref, wcb_ref, wab_ref):
    i = pl.program_id(1)

    @pl.when(i == 0)
    def _():
        wcb_ref[...] = wc_ref[...].astype(BF16)
        wab_ref[...] = wa_ref[...].astype(BF16)

    def emit(a_ref, t_ref):
        ya = jnp.dot(a_ref[...], wcb_ref[...], preferred_element_type=F32)
        yb = jnp.dot(t_ref[...], wab_ref[...], preferred_element_type=F32)
        o_ref[...] = (_sigmoid(gc_ref[...]) * ya + _sigmoid(ga_ref[...]) * yb).astype(BF16)

    @pl.when(i < N_PT)
    def _():
        emit(ap_ref, tp_ref)

    @pl.when(i == N_PT)
    def _():
        emit(as_ref, ts_ref)


def _merge(a_p, a_s, attn_p, attn_s, w_conv_branch, w_attn_branch, h):
    act_p = pl.BlockSpec((TILE, D_CONV), lambda j, i: (jnp.minimum(i, N_PT - 1), 0))
    act_s = pl.BlockSpec((TILE, D_CONV), lambda j, i: (0, 0))
    wsp = pl.BlockSpec((D_CONV, TILE), lambda j, i: (0, j))

    def gate(col):
        return pl.BlockSpec((TILE, TILE), lambda j, i, col=col: (i, col // TILE + j))

    return pl.pallas_call(
        _merge_kernel,
        grid=(D_MODEL // TILE, N_TOK // TILE),
        in_specs=[act_p, act_s, act_p, act_s, wsp, wsp, gate(COL_GC), gate(COL_GA)],
        out_specs=pl.BlockSpec((TILE, TILE), lambda j, i: (i, j)),
        out_shape=jax.ShapeDtypeStruct((N_TOK, D_MODEL), BF16),
        scratch_shapes=[pltpu.VMEM((D_CONV, TILE), BF16), pltpu.VMEM((D_ATTN, TILE), BF16)],
        compiler_params=_params(("arbitrary", "arbitrary")),
        name="merge",
    )(a_p, a_s, attn_p, attn_s, w_conv_branch, w_attn_branch, h, h)


def _out_proj_kernel(m_ref, w_ref, xp_ref, xs_ref, o_ref, wbf_ref):
    i = pl.program_id(1)

    @pl.when(i == 0)
    def _():
        wbf_ref[...] = w_ref[...].astype(BF16)

    y = jnp.dot(m_ref[...], wbf_ref[...], preferred_element_type=F32)

    @pl.when(i < N_PT)
    def _():
        o_ref[...] = ALPHA * xp_ref[...] + y

    @pl.when(i == N_PT)
    def _():
        o_ref[...] = ALPHA * xs_ref[...] + y


def _out_proj(merged, w_o, x_p, x_s):
    return pl.pallas_call(
        _out_proj_kernel,
        grid=(D_MODEL // TILE, N_TOK // TILE),
        in_specs=[pl.BlockSpec((TILE, D_MODEL), lambda j, i: (i, 0)),
                  pl.BlockSpec((D_MODEL, TILE), lambda j, i: (0, j)),
                  pl.BlockSpec((TILE, TILE), lambda j, i: (jnp.minimum(i, N_PT - 1), j)),
                  pl.BlockSpec((TILE, TILE), lambda j, i: (0, j))],
        out_specs=pl.BlockSpec((TILE, TILE), lambda j, i: (i, j)),
        out_shape=jax.ShapeDtypeStruct((N_TOK, D_MODEL), F32),
        scratch_shapes=[pltpu.VMEM((D_MODEL, TILE), BF16)],
        compiler_params=_params(("arbitrary", "arbitrary")),
        name="out_proj",
    )(merged, w_o, x_p, x_s)


def _layer_norm(x, g, b):
    mu = jnp.mean(x, axis=-1, keepdims=True)
    xc = x - mu
    var = jnp.mean(xc * xc, axis=-1, keepdims=True)
    return xc * lax.rsqrt(var + LN_EPS) * g + b


def _rows_to_chunks(x, dst_ref, n_rows, first_chunk=0):
    for c in range(x.shape[1] // LANES):
        dst_ref[pl.ds(first_chunk + c, n_rows, stride=ROW_PITCH), :] = x[:, c * LANES:(c + 1) * LANES]


def _pad_chunks(dst_ref, n_rows):
    zeros = jnp.zeros((n_rows, LANES), F32)
    for c in range(ROW_CHUNKS, ROW_PITCH):
        dst_ref[pl.ds(c, n_rows, stride=ROW_PITCH), :] = zeros


def _chunks_to_rows(src_ref, n_rows):
    return [src_ref[pl.ds(c, n_rows, stride=ROW_PITCH), :] for c in range(ROW_CHUNKS)]


GATHER_UNROLL = 8


def _chunk_row_copy(src_hbm, src_row, buf_ref, dst_row, sem):
    src = pl.multiple_of(src_row * ROW_PITCH, SUBLANES)
    dst = pl.multiple_of(dst_row * ROW_PITCH, SUBLANES)
    return pltpu.make_async_copy(src_hbm.at[pl.ds(src, ROW_CHUNKS)], buf_ref.at[pl.ds(dst, ROW_CHUNKS)], sem)


def _split_bf16(x):
    hi = x.astype(BF16)
    return hi, (x - hi.astype(F32)).astype(BF16)


def _ln_router_kernel(x_ref, g_ref, b_ref, wh_ref, wl_ref, br_ref, h_ref, hc_ref, e_ref, w_ref):
    h = _layer_norm(x_ref[...], g_ref[...], b_ref[...])
    h_ref[...] = h
    _rows_to_chunks(h, hc_ref, LN_TM)
    _pad_chunks(hc_ref, LN_TM)
    h_hi, h_lo = _split_bf16(h)
    w_hi, w_lo = wh_ref[...], wl_ref[...]
    x = (jnp.dot(h_hi, w_hi, preferred_element_type=F32)
         + (jnp.dot(h_lo, w_hi, preferred_element_type=F32) + jnp.dot(h_hi, w_lo, preferred_element_type=F32))
         + br_ref[...])
    lane = lax.broadcasted_iota(jnp.int32, x.shape, 1)
    lane_f = lane.astype(F32)
    ninf = -jnp.inf

    def first_lane(hit):
        return jnp.min(jnp.where(hit, lane_f, float(LANES)), axis=-1, keepdims=True)

    is_g = lane < N_GROUPS
    glog = jnp.where(is_g, x, ninf)
    gmax = jnp.max(glog, axis=-1, keepdims=True)
    grp = first_lane(glog == gmax)
    wg = 1.0 / jnp.sum(jnp.where(is_g, jnp.exp(x - gmax), 0.0), axis=-1, keepdims=True)
    lane_grp = lax.shift_right_logical(lane, 3).astype(F32)
    in_grp = (lane >= N_GROUPS) & (lane < N_GROUPS + N_EXPERTS) & (lane_grp == grp + 1.0)
    elog = jnp.where(in_grp, x, ninf)
    t1 = jnp.max(elog, axis=-1, keepdims=True)
    i1 = first_lane(elog == t1)
    elog2 = jnp.where(lane_f == i1, ninf, elog)
    t2 = jnp.max(elog2, axis=-1, keepdims=True)
    i2 = first_lane(elog2 == t2)
    e2 = jnp.exp(t2 - t1)
    den = 1.0 + e2
    e_ref[...] = jnp.where(lane == 0, i1 - N_GROUPS, jnp.where(lane == 1, i2 - N_GROUPS, 0.0)).astype(jnp.int32)
    w_ref[...] = jnp.where(lane == 0, wg * (1.0 / den), jnp.where(lane == 1, wg * (e2 / den), 0.0))


def _ln_router(pre, g, b, w_router, b_router):
    tm = LN_TM
    row = pl.BlockSpec((tm, D_MODEL), lambda i: (i, 0))
    vec = pl.BlockSpec((1, D_MODEL), lambda i: (0, 0))
    small = pl.BlockSpec((tm, LANES), lambda i: (i, 0))
    wsp = pl.BlockSpec((D_MODEL, LANES), lambda i: (0, 0))
    return pl.pallas_call(
        _ln_router_kernel,
        grid=(N_TOK // tm,),
        in_specs=[row, vec, vec, wsp, wsp,
                  pl.BlockSpec((1, LANES), lambda i: (0, 0))],
        out_specs=[row, pl.BlockSpec((tm * ROW_PITCH, LANES), lambda i: (i, 0)), small, small],
        out_shape=[jax.ShapeDtypeStruct((N_TOK, D_MODEL), F32),
                   jax.ShapeDtypeStruct((N_TOK * ROW_PITCH, LANES), F32),
                   jax.ShapeDtypeStruct((N_TOK, LANES), jnp.int32),
                   jax.ShapeDtypeStruct((N_TOK, LANES), F32)],
        compiler_params=_params(("arbitrary",)),
        name="ln_router",
    )(pre, g, b, *_split_bf16(w_router), b_router)


BUF_ROWS = MOE_TM * ROW_PITCH


def _block_rows(src_ref, cnt_ref, tok_ref, hc_hbm, buf_ref, sem, blk, slot):
    cnt = cnt_ref[blk]
    base = src_ref[blk]
    n_groups = (cnt + GATHER_UNROLL - 1) // GATHER_UNROLL
    slot_ref = buf_ref.at[pl.ds(pl.multiple_of(slot * BUF_ROWS, SUBLANES), BUF_ROWS)]

    def start():
        def issue(g, c):
            for u in range(GATHER_UNROLL):
                r = g * GATHER_UNROLL + u
                tok = tok_ref[base + jnp.minimum(r, cnt - 1)]
                _chunk_row_copy(hc_hbm, tok, slot_ref, r, sem.at[slot]).start(priority=u % 2)
            return c

        lax.fori_loop(0, n_groups, issue, 0)

    def wait():
        def one(g, c):
            n = GATHER_UNROLL * ROW_CHUNKS
            pltpu.make_async_copy(hc_hbm.at[pl.ds(0, n)], slot_ref.at[pl.ds(0, n)], sem.at[slot]).wait()
            return c

        lax.fori_loop(0, n_groups, one, 0)

    return start, wait


def _mlp_up_kernel(nused_ref, e_ref, src_ref, cnt_ref, tok_ref, hc_hbm, wg_ref, wu_ref, o_ref, buf_ref, x_ref, sem):
    del e_ref
    b = pl.program_id(0)
    slot = b % 2

    @pl.when(pl.program_id(1) == 0)
    def _():
        rows = lambda blk, s: _block_rows(src_ref, cnt_ref, tok_ref, hc_hbm, buf_ref, sem, blk, s)

        @pl.when(b == 0)
        def _():
            buf_ref[...] = jnp.zeros_like(buf_ref)
            rows(0, 0)[0]()

        @pl.when(b + 1 < nused_ref[0])
        def _():
            rows(b + 1, 1 - slot)[0]()

        rows(b, slot)[1]()
        live = lax.broadcasted_iota(jnp.int32, (MOE_TM, LANES), 0) < cnt_ref[b]
        first = pl.multiple_of(slot * BUF_ROWS, SUBLANES)
        for c in range(ROW_CHUNKS):
            chunk = buf_ref[pl.ds(first + c, MOE_TM, stride=ROW_PITCH), :]
            x_ref[:, c * LANES:(c + 1) * LANES] = jnp.where(live, chunk, 0.0).astype(BF16)

    x = x_ref[...]
    g = jnp.dot(x, wg_ref[...].astype(BF16), preferred_element_type=F32)
    u = jnp.dot(x, wu_ref[...].astype(BF16), preferred_element_type=F32)
    o_ref[...] = (g * _sigmoid(g) * u).astype(BF16)


def _mlp_up(n_used, blk_e, blk_src, blk_cnt, sorted_tok, h_chunks, w_gate, w_up):
    wsp = pl.BlockSpec((None, D_MODEL, MOE_TF), lambda b, f, nu, e, src, cnt: (e[b], 0, f))
    grid_spec = pltpu.PrefetchScalarGridSpec(
        num_scalar_prefetch=4,
        grid=(n_used[0], D_EXPERT // MOE_TF),
        in_specs=[pl.BlockSpec(memory_space=pltpu.SMEM), pl.BlockSpec(memory_space=pl.ANY), wsp, wsp],
        out_specs=pl.BlockSpec((MOE_TM, MOE_TF), lambda b, f, nu, e, src, cnt: (b, f)),
        scratch_shapes=[pltpu.VMEM((2 * BUF_ROWS, LANES), F32), pltpu.VMEM((MOE_TM, D_MODEL), BF16),
                        pltpu.SemaphoreType.DMA((2,))],
    )
    return pl.pallas_call(
        _mlp_up_kernel,
        grid_spec=grid_spec,
        out_shape=jax.ShapeDtypeStruct((MOE_P, D_EXPERT), BF16),
        compiler_params=_params(("arbitrary", "arbitrary")),
        name="moe_up",
    )(n_used, blk_e, blk_src, blk_cnt, sorted_tok, h_chunks, w_gate, w_up)


def _mlp_down_kernel(nused_ref, e_ref, x_ref, wd_ref, o_ref):
    del nused_ref, e_ref
    o_ref[...] = jnp.dot(x_ref[...], wd_ref[...].astype(BF16), preferred_element_type=F32)


def _mlp_down(n_used, blk_e, hmid, w_down):
    grid_spec = pltpu.PrefetchScalarGridSpec(
        num_scalar_prefetch=2,
        grid=(n_used[0], D_MODEL // MOE_TN),
        in_specs=[pl.BlockSpec((MOE_TM, D_EXPERT), lambda b, n, nu, e: (b, 0)),
                  pl.BlockSpec((None, D_EXPERT, MOE_TN), lambda b, n, nu, e: (e[b], 0, n))],
        out_specs=pl.BlockSpec((MOE_TM, MOE_TN), lambda b, n, nu, e: (b, n)),
    )
    return pl.pallas_call(
        _mlp_down_kernel,
        grid_spec=grid_spec,
        out_shape=jax.ShapeDtypeStruct((MOE_P, D_MODEL), F32),
        compiler_params=_params(("arbitrary", "arbitrary")),
        name="moe_down",
    )(n_used, blk_e, hmid, w_down)


def _combine_kernel(dest_ref, w_ref, h_ref, g_ref, b_ref, ys_hbm, zp_ref, zs_ref, buf_ref, sem):
    i = pl.program_id(0)

    def row_copy(src_row, k, r):
        return pltpu.make_async_copy(ys_hbm.at[pl.ds(src_row, 1)], buf_ref.at[k, pl.ds(r, 1)], sem)

    def issue(g, c):
        for u in range(GATHER_UNROLL):
            r = g * GATHER_UNROLL + u
            for k in range(TOPK_IN_GROUP):
                row_copy(dest_ref[0, TOPK_IN_GROUP * r + k], k, r).start(priority=(u + k) % 2)
        return c

    lax.fori_loop(0, LN_TM // GATHER_UNROLL, issue, 0)
    for k in range(TOPK_IN_GROUP):
        pltpu.make_async_copy(ys_hbm.at[pl.ds(0, LN_TM)], buf_ref.at[k], sem).wait()
    w = w_ref[...]
    moe = buf_ref[0] * w[:, 0:1] + buf_ref[1] * w[:, 1:2]
    z = _layer_norm(ALPHA * h_ref[...] + moe, g_ref[...], b_ref[...])

    @pl.when(i < LN_PT)
    def _():
        zp_ref[...] = z

    @pl.when(i >= LN_PT)
    def _():
        zs_ref[...] = z


def _combine(dest, wts, h1, g, b, ys):
    tm = LN_TM
    row = pl.BlockSpec((tm, D_MODEL), lambda i: (i, 0))
    vec = pl.BlockSpec((1, D_MODEL), lambda i: (0, 0))
    zp, zs = _split_specs(tm, D_MODEL, LN_PT)
    return pl.pallas_call(
        _combine_kernel,
        grid=(N_TOK // tm,),
        in_specs=[pl.BlockSpec((None, 1, TOPK_IN_GROUP * tm), lambda i: (i, 0, 0), memory_space=pltpu.SMEM),
                  pl.BlockSpec((tm, LANES), lambda i: (i, 0)),
                  row, vec, vec,
                  pl.BlockSpec(memory_space=pl.ANY)],
        out_specs=[zp, zs],
        out_shape=[jax.ShapeDtypeStruct((N_P, D_MODEL), F32), jax.ShapeDtypeStruct((N_S, D_MODEL), F32)],
        scratch_shapes=[pltpu.VMEM((TOPK_IN_GROUP, tm, D_MODEL), F32), pltpu.SemaphoreType.DMA(())],
        compiler_params=_params(("arbitrary",)),
        name="moe_combine",
    )(dest.reshape(N_TOK // tm, 1, TOPK_IN_GROUP * tm), wts, h1, g, b, ys)


def _moe_tables(eid):
    flat_e = eid.reshape(-1)
    experts = jnp.arange(N_EXPERTS, dtype=jnp.int32)
    iota_a = jnp.arange(N_ASSIGN, dtype=jnp.int32)
    se, order = lax.sort((flat_e, iota_a), num_keys=1)
    counts = jnp.sum((flat_e[:, None] == experts[None, :]).astype(jnp.int32), axis=0)
    nblk = (counts + MOE_TM - 1) // MOE_TM
    blk_end = jnp.cumsum(nblk)
    blk_start = blk_end - nblk
    start = jnp.cumsum(counts) - counts
    n_used = blk_end[-1]
    shift = blk_start * MOE_TM - start
    dest_sorted = iota_a + jnp.sum(jnp.where(se[:, None] == experts[None, :], shift[None, :], 0), axis=1)
    _, dest = lax.sort((order, dest_sorted), num_keys=1)
    blks = jnp.arange(MOE_NB, dtype=jnp.int32)
    b_e = jnp.minimum(jnp.sum((blk_end[None, :] <= blks[:, None]).astype(jnp.int32), axis=1), N_EXPERTS - 1)
    b_first = (blks - blk_start[b_e]) * MOE_TM
    blk_src = jnp.clip(start[b_e] + b_first, 0, N_ASSIGN - 1)
    blk_cnt = jnp.clip(counts[b_e] - b_first, 1, MOE_TM)
    sorted_tok = order // TOPK_IN_GROUP

    block_tables = (n_used.reshape(1).astype(jnp.int32), b_e.astype(jnp.int32))
    gather_tables = (blk_src.astype(jnp.int32), blk_cnt.astype(jnp.int32), sorted_tok.astype(jnp.int32))
    return block_tables, gather_tables, dest.astype(jnp.int32)


def kernel(x_prompt, x_sample, cache_k, cache_v, state_conv, page_table, w_in, conv_w, w_conv_branch, w_attn_branch, w_o, ln1_g, ln1_b, w_router_group, b_router_group, w_router_expert, b_router_expert, w_gate, w_up, w_down, ln2_g, ln2_b):
    l = 0
    x_p = x_prompt.reshape(N_P, D_MODEL)
    x_s = x_sample.transpose(1, 0, 2).reshape(N_S, D_MODEL)
    h = _in_proj(jnp.concatenate([x_p, x_s], axis=0).astype(BF16), w_in[l])

    a_p, conv_p = _conv_prompt(h, conv_w[l])
    a_s, conv_s_t = _conv_sample(h, state_conv[l].transpose(1, 0, 2), conv_w[l])

    q_p, k_p, v_p, q_s, k_s, v_s = _rope(h, *_rope_tables())
    attn_p = _attn_prompt(q_p, k_p, v_p)

    def seq_major(a, width):
        return a.reshape(DEC_SEQ, DEC_BATCH, width // HEAD_DIM, HEAD_DIM).transpose(1, 0, 2, 3)

    k_s_b, v_s_b = seq_major(k_s, D_KV), seq_major(v_s, D_KV)
    q_s_b = seq_major(q_s, D_ATTN).transpose(0, 2, 1, 3).reshape(DEC_BATCH, ROWS_QS, HEAD_DIM)
    pool_rows = cache_k.shape[1] * PAGE_ROWS
    o_s = _attn_sample(page_table, q_s_b,
                       k_s_b.reshape(DEC_BATCH, NEW_ROWS, HEAD_DIM), v_s_b.reshape(DEC_BATCH, NEW_ROWS, HEAD_DIM),
                       cache_k[l].reshape(pool_rows, HEAD_DIM), cache_v[l].reshape(pool_rows, HEAD_DIM))
    attn_s = (o_s.reshape(DEC_BATCH, N_HEADS, DEC_SEQ, HEAD_DIM).transpose(2, 0, 1, 3)
              .reshape(N_S, D_ATTN).astype(BF16))

    merged = _merge(a_p, a_s, attn_p, attn_s, w_conv_branch[l], w_attn_branch[l], h)
    pre = _out_proj(merged, w_o[l], x_p, x_s)

    w_router = jnp.pad(jnp.concatenate([w_router_group[l], w_router_expert[l]], axis=1),
                       ((0, 0), (0, LANES - N_GROUPS - N_EXPERTS)))
    b_router = jnp.pad(jnp.concatenate([b_router_group[l], b_router_expert[l]]),
                       (0, LANES - N_GROUPS - N_EXPERTS)).reshape(1, LANES)
    h1, h_chunks, eid, wts = _ln_router(pre, ln1_g[l].reshape(1, D_MODEL), ln1_b[l].reshape(1, D_MODEL),
                                        w_router, b_router)
    block_tables, gather_tables, dest = _moe_tables(eid[:, :TOPK_IN_GROUP])
    hmid = _mlp_up(*block_tables, *gather_tables, h_chunks, w_gate[l], w_up[l])
    ys = _mlp_down(*block_tables, hmid, w_down[l])
    z_p, z_s = _combine(dest, wts, h1, ln2_g[l].reshape(1, D_MODEL), ln2_b[l].reshape(1, D_MODEL), ys)

    y_prompt = z_p.reshape(BATCH, SEQ, D_MODEL)
    y_sample = z_s.reshape(DEC_SEQ, DEC_BATCH, D_MODEL).transpose(1, 0, 2)
    k_prompt = k_p.reshape(1, BATCH, SEQ, N_KV_HEADS, HEAD_DIM)
    v_prompt = v_p.reshape(1, BATCH, SEQ, N_KV_HEADS, HEAD_DIM)
    conv_prompt = conv_p[None]
    conv_sample = conv_s_t.transpose(1, 0, 2)[None]
    return (y_prompt, y_sample, k_prompt, v_prompt, conv_prompt, k_s_b[None], v_s_b[None], conv_sample)
```

```python
import jax
import jax.numpy as jnp
from jax import lax
from jax.experimental import pallas as pl
from jax.experimental.pallas import tpu as pltpu

D_MODEL = 4096
BATCH = 4
SEQ = 2048
DEC_BATCH = 128
DEC_SEQ = 4
PAST_LEN = 2048
PAGE_SIZE = 128
N_PAGES = PAST_LEN // PAGE_SIZE
D_CONV = D_MODEL // 2
CONV_WIDTH = 3
HEAD_DIM = 128
N_HEADS = 16
N_KV_HEADS = 4
GROUP = N_HEADS // N_KV_HEADS
D_ATTN = N_HEADS * HEAD_DIM
D_KV = N_KV_HEADS * HEAD_DIM
ROT_DIM = HEAD_DIM // 4
ROPE_THETA = 500000.0
MOBA_BLOCK = 256
MOBA_TOPK = 3
N_GROUPS = 8
EXPERTS_PER_GROUP = 8
N_EXPERTS = N_GROUPS * EXPERTS_PER_GROUP
TOPK_IN_GROUP = 2
D_EXPERT = D_MODEL // 4
DEPTH = 1
ALPHA = (2 * DEPTH) ** 0.25
LN_EPS = 1e-5
D_IN_TOTAL = 3 * D_CONV + D_ATTN + 2 * D_KV + 2 * D_MODEL

N_P = BATCH * SEQ
N_S = DEC_BATCH * DEC_SEQ
N_TOK = N_P + N_S
N_ASSIGN = N_TOK * TOPK_IN_GROUP

COL_CB, COL_CC, COL_CX = 0, D_CONV, 2 * D_CONV
COL_Q = 3 * D_CONV
COL_K = COL_Q + D_ATTN
COL_V = COL_K + D_KV
COL_GC = COL_V + D_KV
COL_GA = COL_GC + D_MODEL

LANES = 128
SUBLANES = 8
TILE = 512
N_PT = N_P // TILE
LN_TM = 256
LN_PT = N_P // LN_TM
MOE_TM = 384
MOE_NB = -(-N_ASSIGN // MOE_TM) + N_EXPERTS
MOE_P = MOE_NB * MOE_TM
MOE_TF = 256
MOE_TN = 2048
ROW_CHUNKS = D_MODEL // LANES
ROW_PITCH = 40
VMEM_LIMIT = 56 * 1024 * 1024

BF16 = jnp.bfloat16
F32 = jnp.float32
_NT = (((1,), (1,)), ((), ()))
_TN = (((0,), (0,)), ((), ()))


def _params(sem, vmem=VMEM_LIMIT):
    return pltpu.CompilerParams(dimension_semantics=sem, vmem_limit_bytes=vmem)


def _sigmoid(x):
    return 1.0 / (1.0 + jnp.exp(-x))


IN_TN = 1024


def _in_proj_kernel(x_ref, w_ref, o_ref, wbf_ref):
    @pl.when(pl.program_id(1) == 0)
    def _():
        wbf_ref[...] = w_ref[...].astype(BF16)

    o_ref[...] = jnp.dot(x_ref[...], wbf_ref[...], preferred_element_type=F32)


def _in_proj(x_bf, w_in):
    return pl.pallas_call(
        _in_proj_kernel,
        grid=(D_IN_TOTAL // IN_TN, N_TOK // TILE),
        in_specs=[pl.BlockSpec((TILE, D_MODEL), lambda j, i: (i, 0)),
                  pl.BlockSpec((D_MODEL, IN_TN), lambda j, i: (0, j))],
        out_specs=pl.BlockSpec((TILE, IN_TN), lambda j, i: (i, j)),
        out_shape=jax.ShapeDtypeStruct((N_TOK, D_IN_TOTAL), F32),
        scratch_shapes=[pltpu.VMEM((D_MODEL, IN_TN), BF16)],
        compiler_params=_params(("arbitrary", "arbitrary"), vmem=60 * 1024 * 1024),
        name="in_proj",
    )(x_bf, w_in)


def _conv_prompt_kernel(cb_ref, cc_ref, cx_ref, w_ref, a_ref, tail_ref, prev_ref):
    @pl.when(pl.program_id(2) == 0)
    def _():
        prev_ref[...] = jnp.zeros_like(prev_ref)

    u = cc_ref[...] * cx_ref[...]
    rows = lax.broadcasted_iota(jnp.int32, u.shape, 0)
    p0 = prev_ref[0:1, :]
    p1 = prev_ref[1:2, :]
    um1 = jnp.where(rows == 0, p1, pltpu.roll(u, 1, 0))
    um2 = jnp.where(rows == 0, p0, jnp.where(rows == 1, p1, pltpu.roll(u, 2, 0)))
    y = w_ref[0:1, :] * um2 + w_ref[1:2, :] * um1 + w_ref[2:3, :] * u
    a_ref[...] = (cb_ref[...] * y).astype(BF16)
    last = u[TILE - 2:TILE, :]
    prev_ref[0:2, :] = last
    tail_ref[...] = last


def _conv_prompt(h, conv_w):
    nr = SEQ // TILE
    nc = D_CONV // TILE

    def sec(col):
        return pl.BlockSpec((TILE, TILE), lambda b, c, r, col=col: (b * nr + r, col // TILE + c))

    return pl.pallas_call(
        _conv_prompt_kernel,
        grid=(BATCH, nc, nr),
        in_specs=[sec(COL_CB), sec(COL_CC), sec(COL_CX),
                  pl.BlockSpec((CONV_WIDTH, TILE), lambda b, c, r: (0, c))],
        out_specs=[pl.BlockSpec((TILE, TILE), lambda b, c, r: (b * nr + r, c)),
                   pl.BlockSpec((None, CONV_WIDTH - 1, TILE), lambda b, c, r: (b, 0, c))],
        out_shape=[jax.ShapeDtypeStruct((N_P, D_CONV), BF16),
                   jax.ShapeDtypeStruct((BATCH, CONV_WIDTH - 1, D_CONV), F32)],
        scratch_shapes=[pltpu.VMEM((SUBLANES, TILE), F32)],
        compiler_params=_params(("arbitrary", "arbitrary", "arbitrary")),
        name="conv_prompt",
    )(h, h, h, conv_w)


def _conv_sample_kernel(cb_ref, cc_ref, cx_ref, st_ref, w_ref, a_ref, tail_ref):
    b = DEC_BATCH
    w0, w1, w2 = w_ref[0:1, :], w_ref[1:2, :], w_ref[2:3, :]
    u = cc_ref[...] * cx_ref[...]
    up = [st_ref[0], st_ref[1]] + [u[t * b:(t + 1) * b, :] for t in range(DEC_SEQ)]
    for t in range(DEC_SEQ):
        y = w0 * up[t] + w1 * up[t + 1] + w2 * up[t + 2]
        a_ref[t * b:(t + 1) * b, :] = (cb_ref[t * b:(t + 1) * b, :] * y).astype(BF16)
    tail_ref[0] = up[DEC_SEQ]
    tail_ref[1] = up[DEC_SEQ + 1]


def _conv_sample(h, state_t, conv_w):
    nc = D_CONV // TILE

    def sec(col):
        return pl.BlockSpec((N_S, TILE), lambda c, col=col: (N_PT, col // TILE + c))

    return pl.pallas_call(
        _conv_sample_kernel,
        grid=(nc,),
        in_specs=[sec(COL_CB), sec(COL_CC), sec(COL_CX),
                  pl.BlockSpec((CONV_WIDTH - 1, DEC_BATCH, TILE), lambda c: (0, 0, c)),
                  pl.BlockSpec((CONV_WIDTH, TILE), lambda c: (0, c))],
        out_specs=[pl.BlockSpec((N_S, TILE), lambda c: (0, c)),
                   pl.BlockSpec((CONV_WIDTH - 1, DEC_BATCH, TILE), lambda c: (0, 0, c))],
        out_shape=[jax.ShapeDtypeStruct((N_S, D_CONV), BF16),
                   jax.ShapeDtypeStruct((CONV_WIDTH - 1, DEC_BATCH, D_CONV), F32)],
        compiler_params=_params(("arbitrary",)),
        name="conv_sample",
    )(h, h, h, state_t, conv_w)


def _rope_kernel(q_ref, k_ref, v_ref, c_ref, s1_ref, s2_ref, qp_ref, kp_ref, vp_ref, qs_ref, ks_ref, vs_ref):
    c, s1, s2 = c_ref[...], s1_ref[...], s2_ref[...]

    def rot(x):
        return x * c + pltpu.roll(x, LANES - ROT_DIM // 2, 1) * s1 + pltpu.roll(x, ROT_DIM // 2, 1) * s2

    def emit(qo_ref, ko_ref, vo_ref):
        for hd in range(N_HEADS):
            sl = slice(hd * HEAD_DIM, (hd + 1) * HEAD_DIM)
            qo_ref[:, sl] = rot(q_ref[:, sl])
        for hd in range(N_KV_HEADS):
            sl = slice(hd * HEAD_DIM, (hd + 1) * HEAD_DIM)
            ko_ref[:, sl] = rot(k_ref[:, sl])
        vo_ref[...] = v_ref[...]

    @pl.when(pl.program_id(0) < N_PT)
    def _():
        emit(qp_ref, kp_ref, vp_ref)

    @pl.when(pl.program_id(0) == N_PT)
    def _():
        emit(qs_ref, ks_ref, vs_ref)


def _split_specs(tm, width, n_prompt_tiles):
    return (pl.BlockSpec((tm, width), lambda i: (jnp.minimum(i, n_prompt_tiles - 1), 0)),
            pl.BlockSpec((tm, width), lambda i: (jnp.maximum(i - n_prompt_tiles, 0), 0)))


def _rope(h, tab_c, tab_s1, tab_s2):
    tiles_per_seq = SEQ // TILE

    def tab_map(i):
        return (jnp.where(i < N_PT, i % tiles_per_seq, tiles_per_seq), 0)

    tab = pl.BlockSpec((TILE, LANES), tab_map)
    qp, qs = _split_specs(TILE, D_ATTN, N_PT)
    kp, ks = _split_specs(TILE, D_KV, N_PT)
    return pl.pallas_call(
        _rope_kernel,
        grid=(N_TOK // TILE,),
        in_specs=[pl.BlockSpec((TILE, D_ATTN), lambda i: (i, COL_Q // D_ATTN)),
                  pl.BlockSpec((TILE, D_KV), lambda i: (i, COL_K // D_KV)),
                  pl.BlockSpec((TILE, D_KV), lambda i: (i, COL_V // D_KV)),
                  tab, tab, tab],
        out_specs=[qp, kp, kp, qs, ks, ks],
        out_shape=[jax.ShapeDtypeStruct((N_P, D_ATTN), F32),
                   jax.ShapeDtypeStruct((N_P, D_KV), F32),
                   jax.ShapeDtypeStruct((N_P, D_KV), F32),
                   jax.ShapeDtypeStruct((N_S, D_ATTN), F32),
                   jax.ShapeDtypeStruct((N_S, D_KV), F32),
                   jax.ShapeDtypeStruct((N_S, D_KV), F32)],
        compiler_params=_params(("arbitrary",)),
        name="rope",
    )(h, h, h, tab_c, tab_s1, tab_s2)


def _rope_tables():
    half = ROT_DIM // 2
    inv = ROPE_THETA ** (-jnp.arange(half, dtype=F32) / half)
    pos = jnp.concatenate([jnp.arange(SEQ, dtype=jnp.int32),
                           PAST_LEN + jnp.repeat(jnp.arange(DEC_SEQ, dtype=jnp.int32), DEC_BATCH)])
    ang = pos.astype(F32)[:, None] * inv[None, :]
    cos, sin = jnp.cos(ang), jnp.sin(ang)
    n = pos.shape[0]
    ones = jnp.ones((n, HEAD_DIM - ROT_DIM), F32)
    zeros = jnp.zeros((n, HEAD_DIM - ROT_DIM), F32)
    zh = jnp.zeros((n, half), F32)
    tab_c = jnp.concatenate([cos, cos, ones], 1)
    tab_s1 = jnp.concatenate([-sin, zh, zeros], 1)
    tab_s2 = jnp.concatenate([zh, sin, zeros], 1)
    return tab_c, tab_s1, tab_s2


ROWS_QP = GROUP * MOBA_BLOCK


def _attn_prompt_kernel(q_ref, k_ref, v_ref, o_ref, kmean_ref, kbf_ref, vt_ref):
    nb = SEQ // MOBA_BLOCK
    qi = pl.program_id(2)

    @pl.when(qi == 0)
    def _():
        k = k_ref[...]
        kmean_ref[...] = jnp.mean(k.reshape(nb, MOBA_BLOCK, HEAD_DIM), axis=1)
        for n in range(nb):
            rows = slice(n * MOBA_BLOCK, (n + 1) * MOBA_BLOCK)
            kbf_ref[n] = k[rows, :].astype(BF16)
            vt_ref[n] = jnp.transpose(v_ref[rows, :]).astype(BF16)

    q = jnp.concatenate([q_ref[:, g * HEAD_DIM:(g + 1) * HEAD_DIM] for g in range(GROUP)], axis=0)
    gate = lax.dot_general(kmean_ref[...], q, _NT, precision=lax.Precision.HIGHEST, preferred_element_type=F32)
    blk = lax.broadcasted_iota(jnp.int32, gate.shape, 0)
    past = blk < qi
    gate = jnp.where(past, gate, -jnp.inf)
    rank = jnp.zeros(gate.shape, jnp.int32)
    for m in range(nb):
        gm = gate[m:m + 1, :]
        beats = (gm > gate) | ((gm == gate) & (blk > m))
        rank = rank + beats.astype(jnp.int32)
    chosen_blocks = jnp.where((rank < MOBA_TOPK) & past, 1.0, 0.0)

    qs = (q * (HEAD_DIM ** -0.5)).astype(BF16)
    s = lax.dot_general(kbf_ref[qi], qs, _NT, preferred_element_type=F32)
    q_off = lax.broadcasted_iota(jnp.int32, s.shape, 1) & (MOBA_BLOCK - 1)
    s = jnp.where(lax.broadcasted_iota(jnp.int32, s.shape, 0) <= q_off, s, -jnp.inf)
    m0 = jnp.max(s, axis=0, keepdims=True)
    p = jnp.exp(s - m0)
    l0 = jnp.sum(p, axis=0, keepdims=True)
    acc0 = jnp.dot(vt_ref[qi], p.astype(BF16), preferred_element_type=F32)

    def body(n, carry):
        m, l, acc = carry
        s = lax.dot_general(kbf_ref[n], qs, _NT, preferred_element_type=F32)
        chosen = jnp.max(jnp.where(blk == n, chosen_blocks, 0.0), axis=0, keepdims=True) > 0.0
        m_new = jnp.where(chosen, jnp.maximum(m, jnp.max(s, axis=0, keepdims=True)), m)
        p = jnp.where(chosen, jnp.exp(s - m_new), 0.0)
        a = jnp.exp(m - m_new)
        l = a * l + jnp.sum(p, axis=0, keepdims=True)
        acc = a * acc + jnp.dot(vt_ref[n], p.astype(BF16), preferred_element_type=F32)
        return m_new, l, acc

    _, l, acc = lax.fori_loop(0, qi, body, (m0, l0, acc0))
    o = jnp.transpose(acc / l).astype(BF16)
    for g in range(GROUP):
        o_ref[:, g * HEAD_DIM:(g + 1) * HEAD_DIM] = o[g * MOBA_BLOCK:(g + 1) * MOBA_BLOCK, :]


def _attn_prompt(q_rot, k_rot, v):
    nq = SEQ // MOBA_BLOCK
    qw = GROUP * HEAD_DIM
    kv_spec = pl.BlockSpec((SEQ, HEAD_DIM), lambda b, kv, qi: (b, kv))
    return pl.pallas_call(
        _attn_prompt_kernel,
        grid=(BATCH, N_KV_HEADS, nq),
        in_specs=[pl.BlockSpec((MOBA_BLOCK, qw), lambda b, kv, qi: (b * nq + qi, kv)), kv_spec, kv_spec],
        out_specs=pl.BlockSpec((MOBA_BLOCK, qw), lambda b, kv, qi: (b * nq + qi, kv)),
        out_shape=jax.ShapeDtypeStruct((N_P, D_ATTN), BF16),
        scratch_shapes=[pltpu.VMEM((SEQ // MOBA_BLOCK, HEAD_DIM), F32),
                        pltpu.VMEM((SEQ // MOBA_BLOCK, MOBA_BLOCK, HEAD_DIM), BF16),
                        pltpu.VMEM((SEQ // MOBA_BLOCK, HEAD_DIM, MOBA_BLOCK), BF16)],
        compiler_params=_params(("arbitrary", "arbitrary", "arbitrary")),
        name="attn_prompt",
    )(q_rot, k_rot, v)


PAGE_ROWS = PAGE_SIZE * N_KV_HEADS
PAST_ROWS = N_PAGES * PAGE_ROWS
NEW_ROWS = DEC_SEQ * N_KV_HEADS
KEY_ROWS = PAST_ROWS + LANES
ROWS_QS = N_KV_HEADS * GROUP * DEC_SEQ
BLOCK_ROWS = MOBA_BLOCK * N_KV_HEADS


def _page_copies(cache_hbm, pt_ref, b, buf_ref, slot, sem):
    return [pltpu.make_async_copy(cache_hbm.at[pl.ds(pl.multiple_of(pt_ref[b * N_PAGES + p] * PAGE_ROWS, PAGE_ROWS),
                                                     PAGE_ROWS)],
                                  buf_ref.at[slot, pl.ds(p * PAGE_ROWS, PAGE_ROWS)], sem.at[slot])
            for p in range(N_PAGES)]


def _attn_sample_kernel(pt_ref, q_ref, kn_ref, vn_ref, ck_hbm, cv_hbm, o_ref, kbuf, vbuf, bias_ref, ksem, vsem):
    b = pl.program_id(0)
    nb = PAST_LEN // MOBA_BLOCK
    slot = b % 2

    @pl.when(b == 0)
    def _():
        for c in _page_copies(ck_hbm, pt_ref, 0, kbuf, 0, ksem) + _page_copies(cv_hbm, pt_ref, 0, vbuf, 0, vsem):
            c.start()
        zeros = jnp.zeros((KEY_ROWS - PAST_ROWS, HEAD_DIM), F32)
        for s in range(2):
            kbuf[s, PAST_ROWS:, :] = zeros
            vbuf[s, PAST_ROWS:, :] = zeros
        key = lax.broadcasted_iota(jnp.int32, (KEY_ROWS, ROWS_QS), 0)
        qrow = lax.broadcasted_iota(jnp.int32, (KEY_ROWS, ROWS_QS), 1)
        same_head = (key & (N_KV_HEADS - 1)) == lax.shift_right_logical(qrow, 4)
        new_t = lax.shift_right_arithmetic(key - PAST_ROWS, 2)
        ok = same_head & ((key < PAST_ROWS) | (new_t <= (qrow & (DEC_SEQ - 1))))
        bias_ref[...] = jnp.where(ok, 0.0, -jnp.inf)

    @pl.when(b + 1 < DEC_BATCH)
    def _():
        nxt = 1 - slot
        for c in (_page_copies(ck_hbm, pt_ref, b + 1, kbuf, nxt, ksem)
                  + _page_copies(cv_hbm, pt_ref, b + 1, vbuf, nxt, vsem)):
            c.start()

    for c in _page_copies(ck_hbm, pt_ref, b, kbuf, slot, ksem) + _page_copies(cv_hbm, pt_ref, b, vbuf, slot, vsem):
        c.wait()
    kbuf[slot, PAST_ROWS:PAST_ROWS + NEW_ROWS, :] = kn_ref[...]
    vbuf[slot, PAST_ROWS:PAST_ROWS + NEW_ROWS, :] = vn_ref[...]

    q = q_ref[...]
    k_all = kbuf[slot]
    ksum = jnp.sum(k_all[:PAST_ROWS].reshape(nb, BLOCK_ROWS // SUBLANES, SUBLANES, HEAD_DIM), axis=1)
    ksum = (ksum + pltpu.roll(ksum, N_KV_HEADS, 1)).reshape(nb * SUBLANES, HEAD_DIM)
    gate = lax.dot_general(ksum, q, _NT, precision=lax.Precision.HIGHEST, preferred_element_type=F32)
    grow = lax.broadcasted_iota(jnp.int32, gate.shape, 0)
    gq = lax.broadcasted_iota(jnp.int32, gate.shape, 1)
    mine = (grow & (SUBLANES - 1)) == lax.shift_right_logical(gq, 4)
    gblk = lax.shift_right_logical(grow, 3)
    gate = jnp.where(mine, gate, -jnp.inf)
    rank = jnp.zeros(gate.shape, jnp.int32)
    for n in range(nb):
        gn = jnp.max(jnp.where(gblk == n, gate, -jnp.inf), axis=0, keepdims=True)
        beats = (gn > gate) | ((gn == gate) & (gblk > n))
        rank = rank + beats.astype(jnp.int32)
    sel = mine & (rank < MOBA_TOPK)
    blk_bias = [jnp.where(jnp.max(jnp.where((gblk == n) & sel, 1.0, 0.0), axis=0, keepdims=True) > 0.0, 0.0, -jnp.inf)
                for n in range(nb)]

    qs = (q * (HEAD_DIM ** -0.5)).astype(BF16)
    s = lax.dot_general(k_all.astype(BF16), qs, _NT, preferred_element_type=F32) + bias_ref[...]
    parts = [s[n * BLOCK_ROWS:(n + 1) * BLOCK_ROWS] + blk_bias[n] for n in range(nb)] + [s[PAST_ROWS:]]
    s = jnp.concatenate(parts, axis=0)
    m = jnp.max(s, axis=0, keepdims=True)
    p = jnp.exp(s - m)
    l = jnp.sum(p, axis=0, keepdims=True)
    p = (p * (1.0 / l)).astype(BF16)
    o_ref[...] = lax.dot_general(p, vbuf[slot].astype(BF16), _TN, preferred_element_type=F32)


def _attn_sample(page_table, q_s, k_new, v_new, cache_k, cache_v):
    q_spec = pl.BlockSpec((None, ROWS_QS, HEAD_DIM), lambda b, pt: (b, 0, 0))
    n_spec = pl.BlockSpec((None, NEW_ROWS, HEAD_DIM), lambda b, pt: (b, 0, 0))
    any_spec = pl.BlockSpec(memory_space=pl.ANY)
    grid_spec = pltpu.PrefetchScalarGridSpec(
        num_scalar_prefetch=1,
        grid=(DEC_BATCH,),
        in_specs=[q_spec, n_spec, n_spec, any_spec, any_spec],
        out_specs=q_spec,
        scratch_shapes=[pltpu.VMEM((2, KEY_ROWS, HEAD_DIM), F32),
                        pltpu.VMEM((2, KEY_ROWS, HEAD_DIM), F32),
                        pltpu.VMEM((KEY_ROWS, ROWS_QS), F32),
                        pltpu.SemaphoreType.DMA((2,)),
                        pltpu.SemaphoreType.DMA((2,))],
    )
    return pl.pallas_call(
        _attn_sample_kernel,
        grid_spec=grid_spec,
        out_shape=jax.ShapeDtypeStruct((DEC_BATCH, ROWS_QS, HEAD_DIM), F32),
        compiler_params=_params(("arbitrary",)),
        name="attn_sample",
    )(page_table.reshape(-1), q_s, k_new, v_new, cache_k, cache_v)


def _merge_kernel(ap_ref, as_ref, tp_ref, ts_ref, wc_ref, wa_ref, gc_ref, ga_ref, o_ref, wcb_ref, wab_ref):
    i = pl.program_id(1)

    @pl.when(i == 0)
    def _():
        wcb_ref[...] = wc_ref[...].astype(BF16)
        wab_ref[...] = wa_ref[...].astype(BF16)

    def emit(a_ref, t_ref):
        ya = jnp.dot(a_ref[...], wcb_ref[...], preferred_element_type=F32)
        yb = jnp.dot(t_ref[...], wab_ref[...], preferred_element_type=F32)
        o_ref[...] = (_sigmoid(gc_ref[...]) * ya + _sigmoid(ga_ref[...]) * yb).astype(BF16)

    @pl.when(i < LN_PT)
    def _():
        emit(ap_ref, tp_ref)

    @pl.when(i >= LN_PT)
    def _():
        emit(as_ref, ts_ref)


def _merge(a_p, a_s, attn_p, attn_s, w_conv_branch, w_attn_branch, h):
    tm, tn = LN_TM, IN_TN
    act_p = pl.BlockSpec((tm, D_CONV), lambda j, i: (jnp.minimum(i, LN_PT - 1), 0))
    act_s = pl.BlockSpec((tm, D_CONV), lambda j, i: (jnp.maximum(i - LN_PT, 0), 0))
    wsp = pl.BlockSpec((D_CONV, tn), lambda j, i: (0, j))

    def gate(col):
        return pl.BlockSpec((tm, tn), lambda j, i, col=col: (i, col // tn + j))

    return pl.pallas_call(
        _merge_kernel,
        grid=(D_MODEL // tn, N_TOK // tm),
        in_specs=[act_p, act_s, act_p, act_s, wsp, wsp, gate(COL_GC), gate(COL_GA)],
        out_specs=pl.BlockSpec((tm, tn), lambda j, i: (i, j)),
        out_shape=jax.ShapeDtypeStruct((N_TOK, D_MODEL), BF16),
        scratch_shapes=[pltpu.VMEM((D_CONV, tn), BF16), pltpu.VMEM((D_ATTN, tn), BF16)],
        compiler_params=_params(("arbitrary", "arbitrary"), vmem=60 * 1024 * 1024),
        name="merge",
    )(a_p, a_s, attn_p, attn_s, w_conv_branch, w_attn_branch, h, h)


def _out_proj_kernel(m_ref, w_ref, xp_ref, xs_ref, o_ref, wbf_ref):
    i = pl.program_id(1)

    @pl.when(i == 0)
    def _():
        wbf_ref[...] = w_ref[...].astype(BF16)

    y = jnp.dot(m_ref[...], wbf_ref[...], preferred_element_type=F32)

    @pl.when(i < LN_PT)
    def _():
        o_ref[...] = ALPHA * xp_ref[...] + y

    @pl.when(i >= LN_PT)
    def _():
        o_ref[...] = ALPHA * xs_ref[...] + y


def _out_proj(merged, w_o, x_p, x_s):
    tm, tn = LN_TM, IN_TN
    return pl.pallas_call(
        _out_proj_kernel,
        grid=(D_MODEL // tn, N_TOK // tm),
        in_specs=[pl.BlockSpec((tm, D_MODEL), lambda j, i: (i, 0)),
                  pl.BlockSpec((D_MODEL, tn), lambda j, i: (0, j)),
                  pl.BlockSpec((tm, tn), lambda j, i: (jnp.minimum(i, LN_PT - 1), j)),
                  pl.BlockSpec((tm, tn), lambda j, i: (jnp.maximum(i - LN_PT, 0), j))],
        out_specs=pl.BlockSpec((tm, tn), lambda j, i: (i, j)),
        out_shape=jax.ShapeDtypeStruct((N_TOK, D_MODEL), F32),
        scratch_shapes=[pltpu.VMEM((D_MODEL, tn), BF16)],
        compiler_params=_params(("arbitrary", "arbitrary"), vmem=60 * 1024 * 1024),
        name="out_proj",
    )(merged, w_o, x_p, x_s)


def _layer_norm(x, g, b):
    mu = jnp.mean(x, axis=-1, keepdims=True)
    xc = x - mu
    var = jnp.mean(xc * xc, axis=-1, keepdims=True)
    return xc * lax.rsqrt(var + LN_EPS) * g + b


def _rows_to_chunks(x, dst_ref, n_rows, first_chunk=0):
    for c in range(x.shape[1] // LANES):
        dst_ref[pl.ds(first_chunk + c, n_rows, stride=ROW_PITCH), :] = x[:, c * LANES:(c + 1) * LANES]


def _pad_chunks(dst_ref, n_rows):
    zeros = jnp.zeros((n_rows, LANES), F32)
    for c in range(ROW_CHUNKS, ROW_PITCH):
        dst_ref[pl.ds(c, n_rows, stride=ROW_PITCH), :] = zeros


def _chunks_to_rows(src_ref, n_rows):
    return [src_ref[pl.ds(c, n_rows, stride=ROW_PITCH), :] for c in range(ROW_CHUNKS)]


GATHER_UNROLL = 8


def _chunk_row_copy(src_hbm, src_row, buf_ref, dst_row, sem):
    src = pl.multiple_of(src_row * ROW_PITCH, SUBLANES)
    dst = pl.multiple_of(dst_row * ROW_PITCH, SUBLANES)
    return pltpu.make_async_copy(src_hbm.at[pl.ds(src, ROW_CHUNKS)], buf_ref.at[pl.ds(dst, ROW_CHUNKS)], sem)


def _split_bf16(x):
    hi = x.astype(BF16)
    return hi, (x - hi.astype(F32)).astype(BF16)


def _ln_router_kernel(x_ref, g_ref, b_ref, wh_ref, wl_ref, br_ref, h_ref, hc_ref, e_ref, w_ref):
    h = _layer_norm(x_ref[...], g_ref[...], b_ref[...])
    h_ref[...] = h
    _rows_to_chunks(h, hc_ref, LN_TM)
    _pad_chunks(hc_ref, LN_TM)
    h_hi, h_lo = _split_bf16(h)
    w_hi, w_lo = wh_ref[...], wl_ref[...]
    x = (jnp.dot(h_hi, w_hi, preferred_element_type=F32)
         + (jnp.dot(h_lo, w_hi, preferred_element_type=F32) + jnp.dot(h_hi, w_lo, preferred_element_type=F32))
         + br_ref[...])
    lane = lax.broadcasted_iota(jnp.int32, x.shape, 1)
    lane_f = lane.astype(F32)
    ninf = -jnp.inf

    def first_lane(hit):
        return jnp.min(jnp.where(hit, lane_f, float(LANES)), axis=-1, keepdims=True)

    is_g = lane < N_GROUPS
    glog = jnp.where(is_g, x, ninf)
    gmax = jnp.max(glog, axis=-1, keepdims=True)
    grp = first_lane(glog == gmax)
    wg = 1.0 / jnp.sum(jnp.where(is_g, jnp.exp(x - gmax), 0.0), axis=-1, keepdims=True)
    lane_grp = lax.shift_right_logical(lane, 3).astype(F32)
    in_grp = (lane >= N_GROUPS) & (lane < N_GROUPS + N_EXPERTS) & (lane_grp == grp + 1.0)
    elog = jnp.where(in_grp, x, ninf)
    t1 = jnp.max(elog, axis=-1, keepdims=True)
    i1 = first_lane(elog == t1)
    elog2 = jnp.where(lane_f == i1, ninf, elog)
    t2 = jnp.max(elog2, axis=-1, keepdims=True)
    i2 = first_lane(elog2 == t2)
    e2 = jnp.exp(t2 - t1)
    den = 1.0 + e2
    e_ref[...] = jnp.where(lane == 0, i1 - N_GROUPS, jnp.where(lane == 1, i2 - N_GROUPS, 0.0)).astype(jnp.int32)
    w_ref[...] = jnp.where(lane == 0, wg * (1.0 / den), jnp.where(lane == 1, wg * (e2 / den), 0.0))


def _ln_router(pre, g, b, w_router, b_router):
    tm = LN_TM
    row = pl.BlockSpec((tm, D_MODEL), lambda i: (i, 0))
    vec = pl.BlockSpec((1, D_MODEL), lambda i: (0, 0))
    small = pl.BlockSpec((tm, LANES), lambda i: (i, 0))
    wsp = pl.BlockSpec((D_MODEL, LANES), lambda i: (0, 0))
    return pl.pallas_call(
        _ln_router_kernel,
        grid=(N_TOK // tm,),
        in_specs=[row, vec, vec, wsp, wsp,
                  pl.BlockSpec((1, LANES), lambda i: (0, 0))],
        out_specs=[row, pl.BlockSpec((tm * ROW_PITCH, LANES), lambda i: (i, 0)), small, small],
        out_shape=[jax.ShapeDtypeStruct((N_TOK, D_MODEL), F32),
                   jax.ShapeDtypeStruct((N_TOK * ROW_PITCH, LANES), F32),
                   jax.ShapeDtypeStruct((N_TOK, LANES), jnp.int32),
                   jax.ShapeDtypeStruct((N_TOK, LANES), F32)],
        compiler_params=_params(("arbitrary",)),
        name="ln_router",
    )(pre, g, b, *_split_bf16(w_router), b_router)


BUF_ROWS = MOE_TM * ROW_PITCH


def _block_rows(src_ref, cnt_ref, tok_ref, hc_hbm, buf_ref, sem, blk, slot):
    cnt = cnt_ref[blk]
    base = src_ref[blk]
    n_groups = (cnt + GATHER_UNROLL - 1) // GATHER_UNROLL
    slot_ref = buf_ref.at[pl.ds(pl.multiple_of(slot * BUF_ROWS, SUBLANES), BUF_ROWS)]

    def start():
        def issue(g, c):
            for u in range(GATHER_UNROLL):
                r = g * GATHER_UNROLL + u
                tok = tok_ref[base + jnp.minimum(r, cnt - 1)]
                _chunk_row_copy(hc_hbm, tok, slot_ref, r, sem.at[slot]).start(priority=u % 2)
            return c

        lax.fori_loop(0, n_groups, issue, 0)

    def wait():
        def one(g, c):
            n = GATHER_UNROLL * ROW_CHUNKS
            pltpu.make_async_copy(hc_hbm.at[pl.ds(0, n)], slot_ref.at[pl.ds(0, n)], sem.at[slot]).wait()
            return c

        lax.fori_loop(0, n_groups, one, 0)

    return start, wait


def _mlp_up_kernel(nused_ref, e_ref, src_ref, cnt_ref, tok_ref, hc_hbm, wg_ref, wu_ref, o_ref, buf_ref, x_ref, sem):
    del e_ref
    b = pl.program_id(0)
    slot = b % 2

    @pl.when(pl.program_id(1) == 0)
    def _():
        rows = lambda blk, s: _block_rows(src_ref, cnt_ref, tok_ref, hc_hbm, buf_ref, sem, blk, s)

        @pl.when(b == 0)
        def _():
            buf_ref[...] = jnp.zeros_like(buf_ref)
            rows(0, 0)[0]()

        @pl.when(b + 1 < nused_ref[0])
        def _():
            rows(b + 1, 1 - slot)[0]()

        rows(b, slot)[1]()
        live = lax.broadcasted_iota(jnp.int32, (MOE_TM, LANES), 0) < cnt_ref[b]
        first = pl.multiple_of(slot * BUF_ROWS, SUBLANES)
        for c in range(ROW_CHUNKS):
            chunk = buf_ref[pl.ds(first + c, MOE_TM, stride=ROW_PITCH), :]
            x_ref[:, c * LANES:(c + 1) * LANES] = jnp.where(live, chunk, 0.0).astype(BF16)

    x = x_ref[...]
    g = jnp.dot(x, wg_ref[...].astype(BF16), preferred_element_type=F32)
    u = jnp.dot(x, wu_ref[...].astype(BF16), preferred_element_type=F32)
    o_ref[...] = (g * _sigmoid(g) * u).astype(BF16)


def _mlp_up(n_used, blk_e, blk_src, blk_cnt, sorted_tok, h_chunks, w_gate, w_up):
    wsp = pl.BlockSpec((None, D_MODEL, MOE_TF), lambda b, f, nu, e, src, cnt: (e[b], 0, f))
    grid_spec = pltpu.PrefetchScalarGridSpec(
        num_scalar_prefetch=4,
        grid=(n_used[0], D_EXPERT // MOE_TF),
        in_specs=[pl.BlockSpec(memory_space=pltpu.SMEM), pl.BlockSpec(memory_space=pl.ANY), wsp, wsp],
        out_specs=pl.BlockSpec((MOE_TM, MOE_TF), lambda b, f, nu, e, src, cnt: (b, f)),
        scratch_shapes=[pltpu.VMEM((2 * BUF_ROWS, LANES), F32), pltpu.VMEM((MOE_TM, D_MODEL), BF16),
                        pltpu.SemaphoreType.DMA((2,))],
    )
    return pl.pallas_call(
        _mlp_up_kernel,
        grid_spec=grid_spec,
        out_shape=jax.ShapeDtypeStruct((MOE_P, D_EXPERT), BF16),
        compiler_params=_params(("arbitrary", "arbitrary")),
        name="moe_up",
    )(n_used, blk_e, blk_src, blk_cnt, sorted_tok, h_chunks, w_gate, w_up)


def _mlp_down_kernel(nused_ref, e_ref, x_ref, wd_ref, o_ref):
    del nused_ref, e_ref
    o_ref[...] = jnp.dot(x_ref[...], wd_ref[...].astype(BF16), preferred_element_type=F32)


def _mlp_down(n_used, blk_e, hmid, w_down):
    grid_spec = pltpu.PrefetchScalarGridSpec(
        num_scalar_prefetch=2,
        grid=(n_used[0], D_MODEL // MOE_TN),
        in_specs=[pl.BlockSpec((MOE_TM, D_EXPERT), lambda b, n, nu, e: (b, 0)),
                  pl.BlockSpec((None, D_EXPERT, MOE_TN), lambda b, n, nu, e: (e[b], 0, n))],
        out_specs=pl.BlockSpec((MOE_TM, MOE_TN), lambda b, n, nu, e: (b, n)),
    )
    return pl.pallas_call(
        _mlp_down_kernel,
        grid_spec=grid_spec,
        out_shape=jax.ShapeDtypeStruct((MOE_P, D_MODEL), F32),
        compiler_params=_params(("arbitrary", "arbitrary")),
        name="moe_down",
    )(n_used, blk_e, hmid, w_down)


def _combine_kernel(dest_ref, w_ref, h_ref, g_ref, b_ref, ys_hbm, zp_ref, zs_ref, buf_ref, sem):
    i = pl.program_id(0)

    def row_copy(src_row, k, r):
        return pltpu.make_async_copy(ys_hbm.at[pl.ds(src_row, 1)], buf_ref.at[k, pl.ds(r, 1)], sem)

    def issue(g, c):
        for u in range(GATHER_UNROLL):
            r = g * GATHER_UNROLL + u
            for k in range(TOPK_IN_GROUP):
                row_copy(dest_ref[0, TOPK_IN_GROUP * r + k], k, r).start(priority=(u + k) % 2)
        return c

    lax.fori_loop(0, LN_TM // GATHER_UNROLL, issue, 0)
    for k in range(TOPK_IN_GROUP):
        pltpu.make_async_copy(ys_hbm.at[pl.ds(0, LN_TM)], buf_ref.at[k], sem).wait()
    w = w_ref[...]
    moe = buf_ref[0] * w[:, 0:1] + buf_ref[1] * w[:, 1:2]
    z = _layer_norm(ALPHA * h_ref[...] + moe, g_ref[...], b_ref[...])

    @pl.when(i < LN_PT)
    def _():
        zp_ref[...] = z

    @pl.when(i >= LN_PT)
    def _():
        zs_ref[...] = z


def _combine(dest, wts, h1, g, b, ys):
    tm = LN_TM
    row = pl.BlockSpec((tm, D_MODEL), lambda i: (i, 0))
    vec = pl.BlockSpec((1, D_MODEL), lambda i: (0, 0))
    zp, zs = _split_specs(tm, D_MODEL, LN_PT)
    return pl.pallas_call(
        _combine_kernel,
        grid=(N_TOK // tm,),
        in_specs=[pl.BlockSpec((None, 1, TOPK_IN_GROUP * tm), lambda i: (i, 0, 0), memory_space=pltpu.SMEM),
                  pl.BlockSpec((tm, LANES), lambda i: (i, 0)),
                  row, vec, vec,
                  pl.BlockSpec(memory_space=pl.ANY)],
        out_specs=[zp, zs],
        out_shape=[jax.ShapeDtypeStruct((N_P, D_MODEL), F32), jax.ShapeDtypeStruct((N_S, D_MODEL), F32)],
        scratch_shapes=[pltpu.VMEM((TOPK_IN_GROUP, tm, D_MODEL), F32), pltpu.SemaphoreType.DMA(())],
        compiler_params=_params(("arbitrary",)),
        name="moe_combine",
    )(dest.reshape(N_TOK // tm, 1, TOPK_IN_GROUP * tm), wts, h1, g, b, ys)


def _moe_tables(eid):
    flat_e = eid.reshape(-1)
    experts = jnp.arange(N_EXPERTS, dtype=jnp.int32)
    iota_a = jnp.arange(N_ASSIGN, dtype=jnp.int32)
    se, order = lax.sort((flat_e, iota_a), num_keys=1)
    counts = jnp.sum((flat_e[:, None] == experts[None, :]).astype(jnp.int32), axis=0)
    nblk = (counts + MOE_TM - 1) // MOE_TM
    blk_end = jnp.cumsum(nblk)
    blk_start = blk_end - nblk
    start = jnp.cumsum(counts) - counts
    n_used = blk_end[-1]
    shift = blk_start * MOE_TM - start
    dest_sorted = iota_a + jnp.sum(jnp.where(se[:, None] == experts[None, :], shift[None, :], 0), axis=1)
    _, dest = lax.sort((order, dest_sorted), num_keys=1)
    blks = jnp.arange(MOE_NB, dtype=jnp.int32)
    b_e = jnp.minimum(jnp.sum((blk_end[None, :] <= blks[:, None]).astype(jnp.int32), axis=1), N_EXPERTS - 1)
    b_first = (blks - blk_start[b_e]) * MOE_TM
    blk_src = jnp.clip(start[b_e] + b_first, 0, N_ASSIGN - 1)
    blk_cnt = jnp.clip(counts[b_e] - b_first, 1, MOE_TM)
    sorted_tok = order // TOPK_IN_GROUP

    block_tables = (n_used.reshape(1).astype(jnp.int32), b_e.astype(jnp.int32))
    gather_tables = (blk_src.astype(jnp.int32), blk_cnt.astype(jnp.int32), sorted_tok.astype(jnp.int32))
    return block_tables, gather_tables, dest.astype(jnp.int32)


def kernel(x_prompt, x_sample, cache_k, cache_v, state_conv, page_table, w_in, conv_w, w_conv_branch, w_attn_branch, w_o, ln1_g, ln1_b, w_router_group, b_router_group, w_router_expert, b_router_expert, w_gate, w_up, w_down, ln2_g, ln2_b):
    l = 0
    x_p = x_prompt.reshape(N_P, D_MODEL)
    x_s = x_sample.transpose(1, 0, 2).reshape(N_S, D_MODEL)
    h = _in_proj(jnp.concatenate([x_p, x_s], axis=0).astype(BF16), w_in[l])

    a_p, conv_p = _conv_prompt(h, conv_w[l])
    a_s, conv_s_t = _conv_sample(h, state_conv[l].transpose(1, 0, 2), conv_w[l])

    q_p, k_p, v_p, q_s, k_s, v_s = _rope(h, *_rope_tables())
    attn_p = _attn_prompt(q_p, k_p, v_p)

    def seq_major(a, width):
        return a.reshape(DEC_SEQ, DEC_BATCH, width // HEAD_DIM, HEAD_DIM).transpose(1, 0, 2, 3)

    k_s_b, v_s_b = seq_major(k_s, D_KV), seq_major(v_s, D_KV)
    q_s_b = seq_major(q_s, D_ATTN).transpose(0, 2, 1, 3).reshape(DEC_BATCH, ROWS_QS, HEAD_DIM)
    pool_rows = cache_k.shape[1] * PAGE_ROWS
    o_s = _attn_sample(page_table, q_s_b,
                       k_s_b.reshape(DEC_BATCH, NEW_ROWS, HEAD_DIM), v_s_b.reshape(DEC_BATCH, NEW_ROWS, HEAD_DIM),
                       cache_k[l].reshape(pool_rows, HEAD_DIM), cache_v[l].reshape(pool_rows, HEAD_DIM))
    attn_s = (o_s.reshape(DEC_BATCH, N_HEADS, DEC_SEQ, HEAD_DIM).transpose(2, 0, 1, 3)
              .reshape(N_S, D_ATTN).astype(BF16))

    merged = _merge(a_p, a_s, attn_p, attn_s, w_conv_branch[l], w_attn_branch[l], h)
    pre = _out_proj(merged, w_o[l], x_p, x_s)

    w_router = jnp.pad(jnp.concatenate([w_router_group[l], w_router_expert[l]], axis=1),
                       ((0, 0), (0, LANES - N_GROUPS - N_EXPERTS)))
    b_router = jnp.pad(jnp.concatenate([b_router_group[l], b_router_expert[l]]),
                       (0, LANES - N_GROUPS - N_EXPERTS)).reshape(1, LANES)
    h1, h_chunks, eid, wts = _ln_router(pre, ln1_g[l].reshape(1, D_MODEL), ln1_b[l].reshape(1, D_MODEL),
                                        w_router, b_router)
    block_tables, gather_tables, dest = _moe_tables(eid[:, :TOPK_IN_GROUP])
    hmid = _mlp_up(*block_tables, *gather_tables, h_chunks, w_gate[l], w_up[l])
    ys = _mlp_down(*block_tables, hmid, w_down[l])
    z_p, z_s = _combine(dest, wts, h1, ln2_g[l].reshape(1, D_MODEL), ln2_b[l].reshape(1, D_MODEL), ys)

    y_prompt = z_p.reshape(BATCH, SEQ, D_MODEL)
    y_sample = z_s.reshape(DEC_SEQ, DEC_BATCH, D_MODEL).transpose(1, 0, 2)
    k_prompt = k_p.reshape(1, BATCH, SEQ, N_KV_HEADS, HEAD_DIM)
    v_prompt = v_p.reshape(1, BATCH, SEQ, N_KV_HEADS, HEAD_DIM)
    conv_prompt = conv_p[None]
    conv_sample = conv_s_t.transpose(1, 0, 2)[None]
    return (y_prompt, y_sample, k_prompt, v_prompt, conv_prompt, k_s_b[None], v_s_b[None], conv_sample)
```

```python
import jax
import jax.numpy as jnp
from jax import lax
from jax.experimental import pallas as pl
from jax.experimental.pallas import tpu as pltpu

D_MODEL = 4096
BATCH = 4
SEQ = 2048
DEC_BATCH = 128
DEC_SEQ = 4
PAST_LEN = 2048
PAGE_SIZE = 128
N_PAGES = PAST_LEN // PAGE_SIZE
D_CONV = D_MODEL // 2
CONV_WIDTH = 3
HEAD_DIM = 128
N_HEADS = 16
N_KV_HEADS = 4
GROUP = N_HEADS // N_KV_HEADS
D_ATTN = N_HEADS * HEAD_DIM
D_KV = N_KV_HEADS * HEAD_DIM
ROT_DIM = HEAD_DIM // 4
ROPE_THETA = 500000.0
MOBA_BLOCK = 256
MOBA_TOPK = 3
N_GROUPS = 8
EXPERTS_PER_GROUP = 8
N_EXPERTS = N_GROUPS * EXPERTS_PER_GROUP
TOPK_IN_GROUP = 2
D_EXPERT = D_MODEL // 4
DEPTH = 1
ALPHA = (2 * DEPTH) ** 0.25
LN_EPS = 1e-5
D_IN_TOTAL = 3 * D_CONV + D_ATTN + 2 * D_KV + 2 * D_MODEL

N_P = BATCH * SEQ
N_S = DEC_BATCH * DEC_SEQ
N_TOK = N_P + N_S
N_ASSIGN = N_TOK * TOPK_IN_GROUP

COL_CB, COL_CC, COL_CX = 0, D_CONV, 2 * D_CONV
COL_Q = 3 * D_CONV
COL_K = COL_Q + D_ATTN
COL_V = COL_K + D_KV
COL_GC = COL_V + D_KV
COL_GA = COL_GC + D_MODEL

LANES = 128
SUBLANES = 8
TILE = 512
N_PT = N_P // TILE
LN_TM = 256
LN_PT = N_P // LN_TM
MOE_TM = 320
MOE_NB = -(-N_ASSIGN // MOE_TM) + N_EXPERTS
MOE_P = MOE_NB * MOE_TM
MOE_TF = 256
MOE_TN = 2048
ROW_CHUNKS = D_MODEL // LANES
ROW_PITCH = 40
VMEM_LIMIT = 56 * 1024 * 1024

BF16 = jnp.bfloat16
F32 = jnp.float32
_NT = (((1,), (1,)), ((), ()))
_TN = (((0,), (0,)), ((), ()))


def _params(sem, vmem=VMEM_LIMIT):
    return pltpu.CompilerParams(dimension_semantics=sem, vmem_limit_bytes=vmem)


def _sigmoid(x):
    return 1.0 / (1.0 + jnp.exp(-x))


IN_TN = 1024


def _in_proj_kernel(x_ref, w_ref, o_ref, wbf_ref):
    @pl.when(pl.program_id(1) == 0)
    def _():
        wbf_ref[...] = w_ref[...].astype(BF16)

    o_ref[...] = jnp.dot(x_ref[...], wbf_ref[...], preferred_element_type=F32)


def _in_proj(x_bf, w_in):
    return pl.pallas_call(
        _in_proj_kernel,
        grid=(D_IN_TOTAL // IN_TN, N_TOK // TILE),
        in_specs=[pl.BlockSpec((TILE, D_MODEL), lambda j, i: (i, 0)),
                  pl.BlockSpec((D_MODEL, IN_TN), lambda j, i: (0, j))],
        out_specs=pl.BlockSpec((TILE, IN_TN), lambda j, i: (i, j)),
        out_shape=jax.ShapeDtypeStruct((N_TOK, D_IN_TOTAL), F32),
        scratch_shapes=[pltpu.VMEM((D_MODEL, IN_TN), BF16)],
        compiler_params=_params(("arbitrary", "arbitrary"), vmem=60 * 1024 * 1024),
        name="in_proj",
    )(x_bf, w_in)


def _conv_prompt_kernel(cb_ref, cc_ref, cx_ref, w_ref, a_ref, tail_ref, prev_ref):
    @pl.when(pl.program_id(2) == 0)
    def _():
        prev_ref[...] = jnp.zeros_like(prev_ref)

    u = cc_ref[...] * cx_ref[...]
    rows = lax.broadcasted_iota(jnp.int32, u.shape, 0)
    p0 = prev_ref[0:1, :]
    p1 = prev_ref[1:2, :]
    um1 = jnp.where(rows == 0, p1, pltpu.roll(u, 1, 0))
    um2 = jnp.where(rows == 0, p0, jnp.where(rows == 1, p1, pltpu.roll(u, 2, 0)))
    y = w_ref[0:1, :] * um2 + w_ref[1:2, :] * um1 + w_ref[2:3, :] * u
    a_ref[...] = (cb_ref[...] * y).astype(BF16)
    last = u[TILE - 2:TILE, :]
    prev_ref[0:2, :] = last
    tail_ref[...] = last


def _conv_prompt(h, conv_w):
    nr = SEQ // TILE
    nc = D_CONV // TILE

    def sec(col):
        return pl.BlockSpec((TILE, TILE), lambda b, c, r, col=col: (b * nr + r, col // TILE + c))

    return pl.pallas_call(
        _conv_prompt_kernel,
        grid=(BATCH, nc, nr),
        in_specs=[sec(COL_CB), sec(COL_CC), sec(COL_CX),
                  pl.BlockSpec((CONV_WIDTH, TILE), lambda b, c, r: (0, c))],
        out_specs=[pl.BlockSpec((TILE, TILE), lambda b, c, r: (b * nr + r, c)),
                   pl.BlockSpec((None, CONV_WIDTH - 1, TILE), lambda b, c, r: (b, 0, c))],
        out_shape=[jax.ShapeDtypeStruct((N_P, D_CONV), BF16),
                   jax.ShapeDtypeStruct((BATCH, CONV_WIDTH - 1, D_CONV), F32)],
        scratch_shapes=[pltpu.VMEM((SUBLANES, TILE), F32)],
        compiler_params=_params(("arbitrary", "arbitrary", "arbitrary")),
        name="conv_prompt",
    )(h, h, h, conv_w)


def _conv_sample_kernel(cb_ref, cc_ref, cx_ref, st_ref, w_ref, a_ref, tail_ref):
    b = DEC_BATCH
    w0, w1, w2 = w_ref[0:1, :], w_ref[1:2, :], w_ref[2:3, :]
    u = cc_ref[...] * cx_ref[...]
    up = [st_ref[0], st_ref[1]] + [u[t * b:(t + 1) * b, :] for t in range(DEC_SEQ)]
    for t in range(DEC_SEQ):
        y = w0 * up[t] + w1 * up[t + 1] + w2 * up[t + 2]
        a_ref[t * b:(t + 1) * b, :] = (cb_ref[t * b:(t + 1) * b, :] * y).astype(BF16)
    tail_ref[0] = up[DEC_SEQ]
    tail_ref[1] = up[DEC_SEQ + 1]


def _conv_sample(h, state_t, conv_w):
    nc = D_CONV // TILE

    def sec(col):
        return pl.BlockSpec((N_S, TILE), lambda c, col=col: (N_PT, col // TILE + c))

    return pl.pallas_call(
        _conv_sample_kernel,
        grid=(nc,),
        in_specs=[sec(COL_CB), sec(COL_CC), sec(COL_CX),
                  pl.BlockSpec((CONV_WIDTH - 1, DEC_BATCH, TILE), lambda c: (0, 0, c)),
                  pl.BlockSpec((CONV_WIDTH, TILE), lambda c: (0, c))],
        out_specs=[pl.BlockSpec((N_S, TILE), lambda c: (0, c)),
                   pl.BlockSpec((CONV_WIDTH - 1, DEC_BATCH, TILE), lambda c: (0, 0, c))],
        out_shape=[jax.ShapeDtypeStruct((N_S, D_CONV), BF16),
                   jax.ShapeDtypeStruct((CONV_WIDTH - 1, DEC_BATCH, D_CONV), F32)],
        compiler_params=_params(("arbitrary",)),
        name="conv_sample",
    )(h, h, h, state_t, conv_w)


def _rope_kernel(q_ref, k_ref, v_ref, c_ref, s1_ref, s2_ref, qp_ref, kp_ref, vp_ref, qs_ref, ks_ref, vs_ref):
    c, s1, s2 = c_ref[...], s1_ref[...], s2_ref[...]

    def rot(x):
        return x * c + pltpu.roll(x, LANES - ROT_DIM // 2, 1) * s1 + pltpu.roll(x, ROT_DIM // 2, 1) * s2

    def emit(qo_ref, ko_ref, vo_ref):
        for hd in range(N_HEADS):
            sl = slice(hd * HEAD_DIM, (hd + 1) * HEAD_DIM)
            qo_ref[:, sl] = rot(q_ref[:, sl])
        for hd in range(N_KV_HEADS):
            sl = slice(hd * HEAD_DIM, (hd + 1) * HEAD_DIM)
            ko_ref[:, sl] = rot(k_ref[:, sl])
        vo_ref[...] = v_ref[...]

    @pl.when(pl.program_id(0) < N_PT)
    def _():
        emit(qp_ref, kp_ref, vp_ref)

    @pl.when(pl.program_id(0) == N_PT)
    def _():
        emit(qs_ref, ks_ref, vs_ref)


def _split_specs(tm, width, n_prompt_tiles):
    return (pl.BlockSpec((tm, width), lambda i: (jnp.minimum(i, n_prompt_tiles - 1), 0)),
            pl.BlockSpec((tm, width), lambda i: (jnp.maximum(i - n_prompt_tiles, 0), 0)))


def _rope(h, tab_c, tab_s1, tab_s2):
    tiles_per_seq = SEQ // TILE

    def tab_map(i):
        return (jnp.where(i < N_PT, i % tiles_per_seq, tiles_per_seq), 0)

    tab = pl.BlockSpec((TILE, LANES), tab_map)
    qp, qs = _split_specs(TILE, D_ATTN, N_PT)
    kp, ks = _split_specs(TILE, D_KV, N_PT)
    return pl.pallas_call(
        _rope_kernel,
        grid=(N_TOK // TILE,),
        in_specs=[pl.BlockSpec((TILE, D_ATTN), lambda i: (i, COL_Q // D_ATTN)),
                  pl.BlockSpec((TILE, D_KV), lambda i: (i, COL_K // D_KV)),
                  pl.BlockSpec((TILE, D_KV), lambda i: (i, COL_V // D_KV)),
                  tab, tab, tab],
        out_specs=[qp, kp, kp, qs, ks, ks],
        out_shape=[jax.ShapeDtypeStruct((N_P, D_ATTN), F32),
                   jax.ShapeDtypeStruct((N_P, D_KV), F32),
                   jax.ShapeDtypeStruct((N_P, D_KV), F32),
                   jax.ShapeDtypeStruct((N_S, D_ATTN), F32),
                   jax.ShapeDtypeStruct((N_S, D_KV), F32),
                   jax.ShapeDtypeStruct((N_S, D_KV), F32)],
        compiler_params=_params(("arbitrary",)),
        name="rope",
    )(h, h, h, tab_c, tab_s1, tab_s2)


def _rope_tables():
    half = ROT_DIM // 2
    inv = ROPE_THETA ** (-jnp.arange(half, dtype=F32) / half)
    pos = jnp.concatenate([jnp.arange(SEQ, dtype=jnp.int32),
                           PAST_LEN + jnp.repeat(jnp.arange(DEC_SEQ, dtype=jnp.int32), DEC_BATCH)])
    ang = pos.astype(F32)[:, None] * inv[None, :]
    cos, sin = jnp.cos(ang), jnp.sin(ang)
    n = pos.shape[0]
    ones = jnp.ones((n, HEAD_DIM - ROT_DIM), F32)
    zeros = jnp.zeros((n, HEAD_DIM - ROT_DIM), F32)
    zh = jnp.zeros((n, half), F32)
    tab_c = jnp.concatenate([cos, cos, ones], 1)
    tab_s1 = jnp.concatenate([-sin, zh, zeros], 1)
    tab_s2 = jnp.concatenate([zh, sin, zeros], 1)
    return tab_c, tab_s1, tab_s2


ROWS_QP = GROUP * MOBA_BLOCK


def _attn_prompt_kernel(q_ref, k_ref, v_ref, o_ref, kmean_ref, kbf_ref, vt_ref):
    nb = SEQ // MOBA_BLOCK
    qi = pl.program_id(2)

    @pl.when(qi == 0)
    def _():
        k = k_ref[...]
        kmean_ref[...] = jnp.mean(k.reshape(nb, MOBA_BLOCK, HEAD_DIM), axis=1)
        for n in range(nb):
            rows = slice(n * MOBA_BLOCK, (n + 1) * MOBA_BLOCK)
            kbf_ref[n] = k[rows, :].astype(BF16)
            vt_ref[n] = jnp.transpose(v_ref[rows, :]).astype(BF16)

    q = jnp.concatenate([q_ref[:, g * HEAD_DIM:(g + 1) * HEAD_DIM] for g in range(GROUP)], axis=0)
    gate = lax.dot_general(kmean_ref[...], q, _NT, precision=lax.Precision.HIGHEST, preferred_element_type=F32)
    blk = lax.broadcasted_iota(jnp.int32, gate.shape, 0)
    past = blk < qi
    gate = jnp.where(past, gate, -jnp.inf)
    rank = jnp.zeros(gate.shape, jnp.int32)
    for m in range(nb):
        gm = gate[m:m + 1, :]
        beats = (gm > gate) | ((gm == gate) & (blk > m))
        rank = rank + beats.astype(jnp.int32)
    chosen_blocks = jnp.where((rank < MOBA_TOPK) & past, 1.0, 0.0)

    qs = (q * (HEAD_DIM ** -0.5)).astype(BF16)
    s = lax.dot_general(kbf_ref[qi], qs, _NT, preferred_element_type=F32)
    q_off = lax.broadcasted_iota(jnp.int32, s.shape, 1) & (MOBA_BLOCK - 1)
    s = jnp.where(lax.broadcasted_iota(jnp.int32, s.shape, 0) <= q_off, s, -jnp.inf)
    m0 = jnp.max(s, axis=0, keepdims=True)
    p = jnp.exp(s - m0)
    l0 = jnp.sum(p, axis=0, keepdims=True)
    acc0 = jnp.dot(vt_ref[qi], p.astype(BF16), preferred_element_type=F32)

    def body(n, carry):
        m, l, acc = carry
        s = lax.dot_general(kbf_ref[n], qs, _NT, preferred_element_type=F32)
        chosen = jnp.max(jnp.where(blk == n, chosen_blocks, 0.0), axis=0, keepdims=True) > 0.0
        m_new = jnp.where(chosen, jnp.maximum(m, jnp.max(s, axis=0, keepdims=True)), m)
        p = jnp.where(chosen, jnp.exp(s - m_new), 0.0)
        a = jnp.exp(m - m_new)
        l = a * l + jnp.sum(p, axis=0, keepdims=True)
        acc = a * acc + jnp.dot(vt_ref[n], p.astype(BF16), preferred_element_type=F32)
        return m_new, l, acc

    _, l, acc = lax.fori_loop(0, qi, body, (m0, l0, acc0))
    o = jnp.transpose(acc / l).astype(BF16)
    for g in range(GROUP):
        o_ref[:, g * HEAD_DIM:(g + 1) * HEAD_DIM] = o[g * MOBA_BLOCK:(g + 1) * MOBA_BLOCK, :]


def _attn_prompt(q_rot, k_rot, v):
    nq = SEQ // MOBA_BLOCK
    qw = GROUP * HEAD_DIM
    kv_spec = pl.BlockSpec((SEQ, HEAD_DIM), lambda b, kv, qi: (b, kv))
    return pl.pallas_call(
        _attn_prompt_kernel,
        grid=(BATCH, N_KV_HEADS, nq),
        in_specs=[pl.BlockSpec((MOBA_BLOCK, qw), lambda b, kv, qi: (b * nq + qi, kv)), kv_spec, kv_spec],
        out_specs=pl.BlockSpec((MOBA_BLOCK, qw), lambda b, kv, qi: (b * nq + qi, kv)),
        out_shape=jax.ShapeDtypeStruct((N_P, D_ATTN), BF16),
        scratch_shapes=[pltpu.VMEM((SEQ // MOBA_BLOCK, HEAD_DIM), F32),
                        pltpu.VMEM((SEQ // MOBA_BLOCK, MOBA_BLOCK, HEAD_DIM), BF16),
                        pltpu.VMEM((SEQ // MOBA_BLOCK, HEAD_DIM, MOBA_BLOCK), BF16)],
        compiler_params=_params(("arbitrary", "arbitrary", "arbitrary")),
        name="attn_prompt",
    )(q_rot, k_rot, v)


PAGE_ROWS = PAGE_SIZE * N_KV_HEADS
PAST_ROWS = N_PAGES * PAGE_ROWS
NEW_ROWS = DEC_SEQ * N_KV_HEADS
KEY_ROWS = PAST_ROWS + LANES
ROWS_QS = N_KV_HEADS * GROUP * DEC_SEQ
BLOCK_ROWS = MOBA_BLOCK * N_KV_HEADS
LOG2_ROWS_PER_KV = (GROUP * DEC_SEQ).bit_length() - 1
LOG2_KV_HEADS = N_KV_HEADS.bit_length() - 1
LOG2_SUBLANES = SUBLANES.bit_length() - 1
LOG2_EXPERTS_PER_GROUP = EXPERTS_PER_GROUP.bit_length() - 1


def _page_copies(cache_hbm, pt_ref, b, buf_ref, slot, sem):
    return [pltpu.make_async_copy(cache_hbm.at[pl.ds(pl.multiple_of(pt_ref[b * N_PAGES + p] * PAGE_ROWS, PAGE_ROWS),
                                                     PAGE_ROWS)],
                                  buf_ref.at[slot, pl.ds(p * PAGE_ROWS, PAGE_ROWS)], sem.at[slot])
            for p in range(N_PAGES)]


def _attn_sample_kernel(pt_ref, q_ref, kn_ref, vn_ref, ck_hbm, cv_hbm, o_ref, kbuf, vbuf, bias_ref, ksem, vsem):
    b = pl.program_id(0)
    nb = PAST_LEN // MOBA_BLOCK
    slot = b % 2

    @pl.when(b == 0)
    def _():
        for c in _page_copies(ck_hbm, pt_ref, 0, kbuf, 0, ksem) + _page_copies(cv_hbm, pt_ref, 0, vbuf, 0, vsem):
            c.start()
        zeros = jnp.zeros((KEY_ROWS - PAST_ROWS, HEAD_DIM), F32)
        for s in range(2):
            kbuf[s, PAST_ROWS:, :] = zeros
            vbuf[s, PAST_ROWS:, :] = zeros
        key = lax.broadcasted_iota(jnp.int32, (KEY_ROWS, ROWS_QS), 0)
        qrow = lax.broadcasted_iota(jnp.int32, (KEY_ROWS, ROWS_QS), 1)
        same_head = (key & (N_KV_HEADS - 1)) == lax.shift_right_logical(qrow, LOG2_ROWS_PER_KV)
        new_t = lax.shift_right_arithmetic(key - PAST_ROWS, LOG2_KV_HEADS)
        ok = same_head & ((key < PAST_ROWS) | (new_t <= (qrow & (DEC_SEQ - 1))))
        bias_ref[...] = jnp.where(ok, 0.0, -jnp.inf)

    @pl.when(b + 1 < DEC_BATCH)
    def _():
        nxt = 1 - slot
        for c in (_page_copies(ck_hbm, pt_ref, b + 1, kbuf, nxt, ksem)
                  + _page_copies(cv_hbm, pt_ref, b + 1, vbuf, nxt, vsem)):
            c.start()

    for c in _page_copies(ck_hbm, pt_ref, b, kbuf, slot, ksem) + _page_copies(cv_hbm, pt_ref, b, vbuf, slot, vsem):
        c.wait()
    kbuf[slot, PAST_ROWS:PAST_ROWS + NEW_ROWS, :] = kn_ref[...]
    vbuf[slot, PAST_ROWS:PAST_ROWS + NEW_ROWS, :] = vn_ref[...]

    q = q_ref[...]
    k_all = kbuf[slot]
    ksum = jnp.sum(k_all[:PAST_ROWS].reshape(nb, BLOCK_ROWS // SUBLANES, SUBLANES, HEAD_DIM), axis=1)
    ksum = (ksum + pltpu.roll(ksum, N_KV_HEADS, 1)).reshape(nb * SUBLANES, HEAD_DIM)
    gate = lax.dot_general(ksum, q, _NT, precision=lax.Precision.HIGHEST, preferred_element_type=F32)
    grow = lax.broadcasted_iota(jnp.int32, gate.shape, 0)
    gq = lax.broadcasted_iota(jnp.int32, gate.shape, 1)
    mine = (grow & (SUBLANES - 1)) == lax.shift_right_logical(gq, LOG2_ROWS_PER_KV)
    gblk = lax.shift_right_logical(grow, LOG2_SUBLANES)
    gate = jnp.where(mine, gate, -jnp.inf)
    rank = jnp.zeros(gate.shape, jnp.int32)
    for n in range(nb):
        gn = jnp.max(jnp.where(gblk == n, gate, -jnp.inf), axis=0, keepdims=True)
        beats = (gn > gate) | ((gn == gate) & (gblk > n))
        rank = rank + beats.astype(jnp.int32)
    sel = mine & (rank < MOBA_TOPK)
    blk_bias = [jnp.where(jnp.max(jnp.where((gblk == n) & sel, 1.0, 0.0), axis=0, keepdims=True) > 0.0, 0.0, -jnp.inf)
                for n in range(nb)]

    qs = (q * (HEAD_DIM ** -0.5)).astype(BF16)
    s = lax.dot_general(k_all.astype(BF16), qs, _NT, preferred_element_type=F32) + bias_ref[...]
    parts = [s[n * BLOCK_ROWS:(n + 1) * BLOCK_ROWS] + blk_bias[n] for n in range(nb)] + [s[PAST_ROWS:]]
    s = jnp.concatenate(parts, axis=0)
    m = jnp.max(s, axis=0, keepdims=True)
    p = jnp.exp(s - m)
    l = jnp.sum(p, axis=0, keepdims=True)
    p = (p * (1.0 / l)).astype(BF16)
    o_ref[...] = lax.dot_general(p, vbuf[slot].astype(BF16), _TN, preferred_element_type=F32)


def _attn_sample(page_table, q_s, k_new, v_new, cache_k, cache_v):
    q_spec = pl.BlockSpec((None, ROWS_QS, HEAD_DIM), lambda b, pt: (b, 0, 0))
    n_spec = pl.BlockSpec((None, NEW_ROWS, HEAD_DIM), lambda b, pt: (b, 0, 0))
    any_spec = pl.BlockSpec(memory_space=pl.ANY)
    grid_spec = pltpu.PrefetchScalarGridSpec(
        num_scalar_prefetch=1,
        grid=(DEC_BATCH,),
        in_specs=[q_spec, n_spec, n_spec, any_spec, any_spec],
        out_specs=q_spec,
        scratch_shapes=[pltpu.VMEM((2, KEY_ROWS, HEAD_DIM), F32),
                        pltpu.VMEM((2, KEY_ROWS, HEAD_DIM), F32),
                        pltpu.VMEM((KEY_ROWS, ROWS_QS), F32),
                        pltpu.SemaphoreType.DMA((2,)),
                        pltpu.SemaphoreType.DMA((2,))],
    )
    return pl.pallas_call(
        _attn_sample_kernel,
        grid_spec=grid_spec,
        out_shape=jax.ShapeDtypeStruct((DEC_BATCH, ROWS_QS, HEAD_DIM), F32),
        compiler_params=_params(("arbitrary",)),
        name="attn_sample",
    )(page_table.reshape(-1), q_s, k_new, v_new, cache_k, cache_v)


def _merge_kernel(ap_ref, as_ref, tp_ref, ts_ref, wc_ref, wa_ref, gc_ref, ga_ref, o_ref, wcb_ref, wab_ref):
    i = pl.program_id(1)

    @pl.when(i == 0)
    def _():
        wcb_ref[...] = wc_ref[...].astype(BF16)
        wab_ref[...] = wa_ref[...].astype(BF16)

    def emit(a_ref, t_ref):
        ya = jnp.dot(a_ref[...], wcb_ref[...], preferred_element_type=F32)
        yb = jnp.dot(t_ref[...], wab_ref[...], preferred_element_type=F32)
        o_ref[...] = (_sigmoid(gc_ref[...]) * ya + _sigmoid(ga_ref[...]) * yb).astype(BF16)

    @pl.when(i < LN_PT)
    def _():
        emit(ap_ref, tp_ref)

    @pl.when(i >= LN_PT)
    def _():
        emit(as_ref, ts_ref)


def _merge(a_p, a_s, attn_p, attn_s, w_conv_branch, w_attn_branch, h):
    tm, tn = LN_TM, IN_TN
    act_p = pl.BlockSpec((tm, D_CONV), lambda j, i: (jnp.minimum(i, LN_PT - 1), 0))
    act_s = pl.BlockSpec((tm, D_CONV), lambda j, i: (jnp.maximum(i - LN_PT, 0), 0))
    wsp = pl.BlockSpec((D_CONV, tn), lambda j, i: (0, j))

    def gate(col):
        return pl.BlockSpec((tm, tn), lambda j, i, col=col: (i, col // tn + j))

    return pl.pallas_call(
        _merge_kernel,
        grid=(D_MODEL // tn, N_TOK // tm),
        in_specs=[act_p, act_s, act_p, act_s, wsp, wsp, gate(COL_GC), gate(COL_GA)],
        out_specs=pl.BlockSpec((tm, tn), lambda j, i: (i, j)),
        out_shape=jax.ShapeDtypeStruct((N_TOK, D_MODEL), BF16),
        scratch_shapes=[pltpu.VMEM((D_CONV, tn), BF16), pltpu.VMEM((D_ATTN, tn), BF16)],
        compiler_params=_params(("arbitrary", "arbitrary"), vmem=60 * 1024 * 1024),
        name="merge",
    )(a_p, a_s, attn_p, attn_s, w_conv_branch, w_attn_branch, h, h)


def _out_proj_kernel(m_ref, w_ref, xp_ref, xs_ref, o_ref, wbf_ref):
    i = pl.program_id(1)

    @pl.when(i == 0)
    def _():
        wbf_ref[...] = w_ref[...].astype(BF16)

    y = jnp.dot(m_ref[...], wbf_ref[...], preferred_element_type=F32)

    @pl.when(i < LN_PT)
    def _():
        o_ref[...] = ALPHA * xp_ref[...] + y

    @pl.when(i >= LN_PT)
    def _():
        o_ref[...] = ALPHA * xs_ref[...] + y


def _out_proj(merged, w_o, x_p, x_s):
    tm, tn = LN_TM, IN_TN
    return pl.pallas_call(
        _out_proj_kernel,
        grid=(D_MODEL // tn, N_TOK // tm),
        in_specs=[pl.BlockSpec((tm, D_MODEL), lambda j, i: (i, 0)),
                  pl.BlockSpec((D_MODEL, tn), lambda j, i: (0, j)),
                  pl.BlockSpec((tm, tn), lambda j, i: (jnp.minimum(i, LN_PT - 1), j)),
                  pl.BlockSpec((tm, tn), lambda j, i: (jnp.maximum(i - LN_PT, 0), j))],
        out_specs=pl.BlockSpec((tm, tn), lambda j, i: (i, j)),
        out_shape=jax.ShapeDtypeStruct((N_TOK, D_MODEL), F32),
        scratch_shapes=[pltpu.VMEM((D_MODEL, tn), BF16)],
        compiler_params=_params(("arbitrary", "arbitrary"), vmem=60 * 1024 * 1024),
        name="out_proj",
    )(merged, w_o, x_p, x_s)


def _layer_norm(x, g, b):
    mu = jnp.mean(x, axis=-1, keepdims=True)
    xc = x - mu
    var = jnp.mean(xc * xc, axis=-1, keepdims=True)
    return xc * lax.rsqrt(var + LN_EPS) * g + b


def _rows_to_chunks(x, dst_ref, n_rows, first_chunk=0):
    for c in range(x.shape[1] // LANES):
        dst_ref[pl.ds(first_chunk + c, n_rows, stride=ROW_PITCH), :] = x[:, c * LANES:(c + 1) * LANES]


def _pad_chunks(dst_ref, n_rows):
    zeros = jnp.zeros((n_rows, LANES), F32)
    for c in range(ROW_CHUNKS, ROW_PITCH):
        dst_ref[pl.ds(c, n_rows, stride=ROW_PITCH), :] = zeros


def _chunks_to_rows(src_ref, n_rows):
    return [src_ref[pl.ds(c, n_rows, stride=ROW_PITCH), :] for c in range(ROW_CHUNKS)]


GATHER_UNROLL = 8


def _chunk_row_copy(src_hbm, src_row, buf_ref, dst_row, sem):
    src = pl.multiple_of(src_row * ROW_PITCH, SUBLANES)
    dst = pl.multiple_of(dst_row * ROW_PITCH, SUBLANES)
    return pltpu.make_async_copy(src_hbm.at[pl.ds(src, ROW_CHUNKS)], buf_ref.at[pl.ds(dst, ROW_CHUNKS)], sem)


def _split_bf16(x):
    hi = x.astype(BF16)
    return hi, (x - hi.astype(F32)).astype(BF16)


def _ln_router_kernel(x_ref, g_ref, b_ref, wh_ref, wl_ref, br_ref, h_ref, hc_ref, e_ref, w_ref):
    h = _layer_norm(x_ref[...], g_ref[...], b_ref[...])
    h_ref[...] = h
    _rows_to_chunks(h, hc_ref, LN_TM)
    _pad_chunks(hc_ref, LN_TM)
    h_hi, h_lo = _split_bf16(h)
    w_hi, w_lo = wh_ref[...], wl_ref[...]
    x = (jnp.dot(h_hi, w_hi, preferred_element_type=F32)
         + (jnp.dot(h_lo, w_hi, preferred_element_type=F32) + jnp.dot(h_hi, w_lo, preferred_element_type=F32))
         + br_ref[...])
    lane = lax.broadcasted_iota(jnp.int32, x.shape, 1)
    lane_f = lane.astype(F32)
    ninf = -jnp.inf

    def first_lane(hit):
        return jnp.min(jnp.where(hit, lane_f, float(LANES)), axis=-1, keepdims=True)

    is_g = lane < N_GROUPS
    glog = jnp.where(is_g, x, ninf)
    gmax = jnp.max(glog, axis=-1, keepdims=True)
    grp = first_lane(glog == gmax)
    wg = 1.0 / jnp.sum(jnp.where(is_g, jnp.exp(x - gmax), 0.0), axis=-1, keepdims=True)
    lane_grp = lax.shift_right_logical(lane, LOG2_EXPERTS_PER_GROUP).astype(F32)
    in_grp = (lane >= N_GROUPS) & (lane < N_GROUPS + N_EXPERTS) & (lane_grp == grp + 1.0)
    elog = jnp.where(in_grp, x, ninf)
    t1 = jnp.max(elog, axis=-1, keepdims=True)
    i1 = first_lane(elog == t1)
    elog2 = jnp.where(lane_f == i1, ninf, elog)
    t2 = jnp.max(elog2, axis=-1, keepdims=True)
    i2 = first_lane(elog2 == t2)
    e2 = jnp.exp(t2 - t1)
    den = 1.0 + e2
    e_ref[...] = jnp.where(lane == 0, i1 - N_GROUPS, jnp.where(lane == 1, i2 - N_GROUPS, 0.0)).astype(jnp.int32)
    w_ref[...] = jnp.where(lane == 0, wg * (1.0 / den), jnp.where(lane == 1, wg * (e2 / den), 0.0))


def _ln_router(pre, g, b, w_router, b_router):
    tm = LN_TM
    row = pl.BlockSpec((tm, D_MODEL), lambda i: (i, 0))
    vec = pl.BlockSpec((1, D_MODEL), lambda i: (0, 0))
    small = pl.BlockSpec((tm, LANES), lambda i: (i, 0))
    wsp = pl.BlockSpec((D_MODEL, LANES), lambda i: (0, 0))
    return pl.pallas_call(
        _ln_router_kernel,
        grid=(N_TOK // tm,),
        in_specs=[row, vec, vec, wsp, wsp,
                  pl.BlockSpec((1, LANES), lambda i: (0, 0))],
        out_specs=[row, pl.BlockSpec((tm * ROW_PITCH, LANES), lambda i: (i, 0)), small, small],
        out_shape=[jax.ShapeDtypeStruct((N_TOK, D_MODEL), F32),
                   jax.ShapeDtypeStruct((N_TOK * ROW_PITCH, LANES), F32),
                   jax.ShapeDtypeStruct((N_TOK, LANES), jnp.int32),
                   jax.ShapeDtypeStruct((N_TOK, LANES), F32)],
        compiler_params=_params(("arbitrary",)),
        name="ln_router",
    )(pre, g, b, *_split_bf16(w_router), b_router)


BUF_ROWS = MOE_TM * ROW_PITCH


def _block_rows(src_ref, cnt_ref, tok_ref, hc_hbm, buf_ref, sem, blk, slot):
    cnt = cnt_ref[blk]
    base = src_ref[blk]
    n_groups = (cnt + GATHER_UNROLL - 1) // GATHER_UNROLL
    slot_ref = buf_ref.at[pl.ds(pl.multiple_of(slot * BUF_ROWS, SUBLANES), BUF_ROWS)]

    def start():
        def issue(g, c):
            for u in range(GATHER_UNROLL):
                r = g * GATHER_UNROLL + u
                tok = tok_ref[base + jnp.minimum(r, cnt - 1)]
                _chunk_row_copy(hc_hbm, tok, slot_ref, r, sem.at[slot]).start(priority=u % 2)
            return c

        lax.fori_loop(0, n_groups, issue, 0)

    def wait():
        def one(g, c):
            n = GATHER_UNROLL * ROW_CHUNKS
            pltpu.make_async_copy(hc_hbm.at[pl.ds(0, n)], slot_ref.at[pl.ds(0, n)], sem.at[slot]).wait()
            return c

        lax.fori_loop(0, n_groups, one, 0)

    return start, wait


def _mlp_up_kernel(nused_ref, e_ref, src_ref, cnt_ref, tok_ref, hc_hbm, wg_ref, wu_ref, o_ref, buf_ref, x_ref, sem):
    del e_ref
    b = pl.program_id(0)
    slot = b % 2

    @pl.when(pl.program_id(1) == 0)
    def _():
        rows = lambda blk, s: _block_rows(src_ref, cnt_ref, tok_ref, hc_hbm, buf_ref, sem, blk, s)

        @pl.when(b == 0)
        def _():
            buf_ref[...] = jnp.zeros_like(buf_ref)
            rows(0, 0)[0]()

        @pl.when(b + 1 < nused_ref[0])
        def _():
            rows(b + 1, 1 - slot)[0]()

        rows(b, slot)[1]()
        live = lax.broadcasted_iota(jnp.int32, (MOE_TM, LANES), 0) < cnt_ref[b]
        first = pl.multiple_of(slot * BUF_ROWS, SUBLANES)
        for c in range(ROW_CHUNKS):
            chunk = buf_ref[pl.ds(first + c, MOE_TM, stride=ROW_PITCH), :]
            x_ref[:, c * LANES:(c + 1) * LANES] = jnp.where(live, chunk, 0.0).astype(BF16)

    x = x_ref[...]
    g = jnp.dot(x, wg_ref[...].astype(BF16), preferred_element_type=F32)
    u = jnp.dot(x, wu_ref[...].astype(BF16), preferred_element_type=F32)
    o_ref[...] = (g * _sigmoid(g) * u).astype(BF16)


def _mlp_up(n_used, blk_e, blk_src, blk_cnt, sorted_tok, h_chunks, w_gate, w_up):
    wsp = pl.BlockSpec((None, D_MODEL, MOE_TF), lambda b, f, nu, e, src, cnt: (e[b], 0, f))
    grid_spec = pltpu.PrefetchScalarGridSpec(
        num_scalar_prefetch=4,
        grid=(n_used[0], D_EXPERT // MOE_TF),
        in_specs=[pl.BlockSpec(memory_space=pltpu.SMEM), pl.BlockSpec(memory_space=pl.ANY), wsp, wsp],
        out_specs=pl.BlockSpec((MOE_TM, MOE_TF), lambda b, f, nu, e, src, cnt: (b, f)),
        scratch_shapes=[pltpu.VMEM((2 * BUF_ROWS, LANES), F32), pltpu.VMEM((MOE_TM, D_MODEL), BF16),
                        pltpu.SemaphoreType.DMA((2,))],
    )
    return pl.pallas_call(
        _mlp_up_kernel,
        grid_spec=grid_spec,
        out_shape=jax.ShapeDtypeStruct((MOE_P, D_EXPERT), BF16),
        compiler_params=_params(("arbitrary", "arbitrary")),
        name="moe_up",
    )(n_used, blk_e, blk_src, blk_cnt, sorted_tok, h_chunks, w_gate, w_up)


def _mlp_down_kernel(nused_ref, e_ref, x_ref, wd_ref, o_ref):
    del nused_ref, e_ref
    o_ref[...] = jnp.dot(x_ref[...], wd_ref[...].astype(BF16), preferred_element_type=F32)


def _mlp_down(n_used, blk_e, hmid, w_down):
    grid_spec = pltpu.PrefetchScalarGridSpec(
        num_scalar_prefetch=2,
        grid=(n_used[0], D_MODEL // MOE_TN),
        in_specs=[pl.BlockSpec((MOE_TM, D_EXPERT), lambda b, n, nu, e: (b, 0)),
                  pl.BlockSpec((None, D_EXPERT, MOE_TN), lambda b, n, nu, e: (e[b], 0, n))],
        out_specs=pl.BlockSpec((MOE_TM, MOE_TN), lambda b, n, nu, e: (b, n)),
    )
    return pl.pallas_call(
        _mlp_down_kernel,
        grid_spec=grid_spec,
        out_shape=jax.ShapeDtypeStruct((MOE_P, D_MODEL), F32),
        compiler_params=_params(("arbitrary", "arbitrary")),
        name="moe_down",
    )(n_used, blk_e, hmid, w_down)


def _combine_kernel(dest_ref, w_ref, h_ref, g_ref, b_ref, ys_hbm, zp_ref, zs_ref, buf_ref, sem):
    i = pl.program_id(0)

    def row_copy(src_row, k, r):
        return pltpu.make_async_copy(ys_hbm.at[pl.ds(src_row, 1)], buf_ref.at[k, pl.ds(r, 1)], sem)

    def issue(g, c):
        for u in range(GATHER_UNROLL):
            r = g * GATHER_UNROLL + u
            for k in range(TOPK_IN_GROUP):
                row_copy(dest_ref[0, TOPK_IN_GROUP * r + k], k, r).start(priority=(u + k) % 2)
        return c

    lax.fori_loop(0, LN_TM // GATHER_UNROLL, issue, 0)
    for k in range(TOPK_IN_GROUP):
        pltpu.make_async_copy(ys_hbm.at[pl.ds(0, LN_TM)], buf_ref.at[k], sem).wait()
    w = w_ref[...]
    moe = buf_ref[0] * w[:, 0:1] + buf_ref[1] * w[:, 1:2]
    z = _layer_norm(ALPHA * h_ref[...] + moe, g_ref[...], b_ref[...])

    @pl.when(i < LN_PT)
    def _():
        zp_ref[...] = z

    @pl.when(i >= LN_PT)
    def _():
        zs_ref[...] = z


def _combine(dest, wts, h1, g, b, ys):
    tm = LN_TM
    row = pl.BlockSpec((tm, D_MODEL), lambda i: (i, 0))
    vec = pl.BlockSpec((1, D_MODEL), lambda i: (0, 0))
    zp, zs = _split_specs(tm, D_MODEL, LN_PT)
    return pl.pallas_call(
        _combine_kernel,
        grid=(N_TOK // tm,),
        in_specs=[pl.BlockSpec((None, 1, TOPK_IN_GROUP * tm), lambda i: (i, 0, 0), memory_space=pltpu.SMEM),
                  pl.BlockSpec((tm, LANES), lambda i: (i, 0)),
                  row, vec, vec,
                  pl.BlockSpec(memory_space=pl.ANY)],
        out_specs=[zp, zs],
        out_shape=[jax.ShapeDtypeStruct((N_P, D_MODEL), F32), jax.ShapeDtypeStruct((N_S, D_MODEL), F32)],
        scratch_shapes=[pltpu.VMEM((TOPK_IN_GROUP, tm, D_MODEL), F32), pltpu.SemaphoreType.DMA(())],
        compiler_params=_params(("arbitrary",)),
        name="moe_combine",
    )(dest.reshape(N_TOK // tm, 1, TOPK_IN_GROUP * tm), wts, h1, g, b, ys)


def _moe_tables(eid):
    flat_e = eid.reshape(-1)
    experts = jnp.arange(N_EXPERTS, dtype=jnp.int32)
    iota_a = jnp.arange(N_ASSIGN, dtype=jnp.int32)
    se, order = lax.sort((flat_e, iota_a), num_keys=1)
    counts = jnp.sum((flat_e[:, None] == experts[None, :]).astype(jnp.int32), axis=0)
    nblk = (counts + MOE_TM - 1) // MOE_TM
    blk_end = jnp.cumsum(nblk)
    blk_start = blk_end - nblk
    start = jnp.cumsum(counts) - counts
    n_used = blk_end[-1]
    shift = blk_start * MOE_TM - start
    dest_sorted = iota_a + jnp.sum(jnp.where(se[:, None] == experts[None, :], shift[None, :], 0), axis=1)
    _, dest = lax.sort((order, dest_sorted), num_keys=1)
    blks = jnp.arange(MOE_NB, dtype=jnp.int32)
    b_e = jnp.minimum(jnp.sum((blk_end[None, :] <= blks[:, None]).astype(jnp.int32), axis=1), N_EXPERTS - 1)
    b_first = (blks - blk_start[b_e]) * MOE_TM
    blk_src = jnp.clip(start[b_e] + b_first, 0, N_ASSIGN - 1)
    blk_cnt = jnp.clip(counts[b_e] - b_first, 1, MOE_TM)
    sorted_tok = order // TOPK_IN_GROUP

    block_tables = (n_used.reshape(1).astype(jnp.int32), b_e.astype(jnp.int32))
    gather_tables = (blk_src.astype(jnp.int32), blk_cnt.astype(jnp.int32), sorted_tok.astype(jnp.int32))
    return block_tables, gather_tables, dest.astype(jnp.int32)


def kernel(x_prompt, x_sample, cache_k, cache_v, state_conv, page_table, w_in, conv_w, w_conv_branch, w_attn_branch, w_o, ln1_g, ln1_b, w_router_group, b_router_group, w_router_expert, b_router_expert, w_gate, w_up, w_down, ln2_g, ln2_b):
    l = 0
    x_p = x_prompt.reshape(N_P, D_MODEL)
    x_s = x_sample.transpose(1, 0, 2).reshape(N_S, D_MODEL)
    h = _in_proj(jnp.concatenate([x_p, x_s], axis=0).astype(BF16), w_in[l])

    a_p, conv_p = _conv_prompt(h, conv_w[l])
    a_s, conv_s_t = _conv_sample(h, state_conv[l].transpose(1, 0, 2), conv_w[l])

    q_p, k_p, v_p, q_s, k_s, v_s = _rope(h, *_rope_tables())
    attn_p = _attn_prompt(q_p, k_p, v_p)

    def seq_major(a, width):
        return a.reshape(DEC_SEQ, DEC_BATCH, width // HEAD_DIM, HEAD_DIM).transpose(1, 0, 2, 3)

    k_s_b, v_s_b = seq_major(k_s, D_KV), seq_major(v_s, D_KV)
    q_s_b = seq_major(q_s, D_ATTN).transpose(0, 2, 1, 3).reshape(DEC_BATCH, ROWS_QS, HEAD_DIM)
    pool_rows = cache_k.shape[1] * PAGE_ROWS
    o_s = _attn_sample(page_table, q_s_b,
                       k_s_b.reshape(DEC_BATCH, NEW_ROWS, HEAD_DIM), v_s_b.reshape(DEC_BATCH, NEW_ROWS, HEAD_DIM),
                       cache_k[l].reshape(pool_rows, HEAD_DIM), cache_v[l].reshape(pool_rows, HEAD_DIM))
    attn_s = (o_s.reshape(DEC_BATCH, N_HEADS, DEC_SEQ, HEAD_DIM).transpose(2, 0, 1, 3)
              .reshape(N_S, D_ATTN).astype(BF16))

    merged = _merge(a_p, a_s, attn_p, attn_s, w_conv_branch[l], w_attn_branch[l], h)
    pre = _out_proj(merged, w_o[l], x_p, x_s)

    w_router = jnp.pad(jnp.concatenate([w_router_group[l], w_router_expert[l]], axis=1),
                       ((0, 0), (0, LANES - N_GROUPS - N_EXPERTS)))
    b_router = jnp.pad(jnp.concatenate([b_router_group[l], b_router_expert[l]]),
                       (0, LANES - N_GROUPS - N_EXPERTS)).reshape(1, LANES)
    h1, h_chunks, eid, wts = _ln_router(pre, ln1_g[l].reshape(1, D_MODEL), ln1_b[l].reshape(1, D_MODEL),
                                        w_router, b_router)
    block_tables, gather_tables, dest = _moe_tables(eid[:, :TOPK_IN_GROUP])
    hmid = _mlp_up(*block_tables, *gather_tables, h_chunks, w_gate[l], w_up[l])
    ys = _mlp_down(*block_tables, hmid, w_down[l])
    z_p, z_s = _combine(dest, wts, h1, ln2_g[l].reshape(1, D_MODEL), ln2_b[l].reshape(1, D_MODEL), ys)

    y_prompt = z_p.reshape(BATCH, SEQ, D_MODEL)
    y_sample = z_s.reshape(DEC_SEQ, DEC_BATCH, D_MODEL).transpose(1, 0, 2)
    k_prompt = k_p.reshape(1, BATCH, SEQ, N_KV_HEADS, HEAD_DIM)
    v_prompt = v_p.reshape(1, BATCH, SEQ, N_KV_HEADS, HEAD_DIM)
    conv_prompt = conv_p[None]
    conv_sample = conv_s_t.transpose(1, 0, 2)[None]
    return (y_prompt, y_sample, k_prompt, v_prompt, conv_prompt, k_s_b[None], v_s_b[None], conv_sample)
```
